```python
import jax, jax.numpy as jnp
from jax import lax
import numpy as np

D_MODEL = 2048
BATCH = 2
SEQ = 4096
DEPTH = 2
DEC_BATCH = 128
DEC_SEQ = 4
PAST_LEN = 2048
PAGE_SIZE = 128

HEAD_DIM = 128
N_SLOTS = 8
DIL_GROUPS = ((128, 1), (512, 4), (2048, 16))
N_GROUPS = 3
ATTN_QKV = N_GROUPS * N_SLOTS * HEAD_DIM
ATTN_OUT = N_SLOTS * HEAD_DIM
D_CONV = D_MODEL // 2
CONV_WIDTH = 31
D_FF = 4 * D_MODEL
FFN_CONV_WIDTH = 3
D_PLE = 256
ROPE_THETA = 10000.0
EPS = 1e-6
NEG = -1e30
IN_COLS = 3 * ATTN_QKV + 2 * D_CONV + 2 * D_MODEL

kernel_name = "hybrid_dilated_attn_conformer_step"


def rmsnorm(x, g):
    xf = x.astype(jnp.float32)
    y = xf * lax.rsqrt(jnp.mean(xf * xf, axis=-1, keepdims=True) + EPS)
    return (y * g.astype(jnp.float32)).astype(x.dtype)


def layernorm(x, g, b):
    xf = x.astype(jnp.float32)
    mu = jnp.mean(xf, axis=-1, keepdims=True)
    var = jnp.mean(jnp.square(xf - mu), axis=-1, keepdims=True)
    y = (xf - mu) * lax.rsqrt(var + EPS)
    return (y * g.astype(jnp.float32) + b.astype(jnp.float32)).astype(x.dtype)


def rope(x, pos):
    half = HEAD_DIM // 2
    inv = ROPE_THETA ** (-jnp.arange(half, dtype=jnp.float32) / half)
    ang = pos.astype(jnp.float32)[:, None] * inv[None, :]
    cos = jnp.cos(ang)[None, :, None, :]
    sin = jnp.sin(ang)[None, :, None, :]
    x1 = x[..., :half].astype(jnp.float32)
    x2 = x[..., half:].astype(jnp.float32)
    return jnp.concatenate([x1 * cos - x2 * sin, x2 * cos + x1 * sin], axis=-1).astype(x.dtype)


def causal_dwconv(x_hist, w, b):
    c = x_hist.shape[-1]
    out = lax.conv_general_dilated(x_hist, w[:, None, :].astype(x_hist.dtype), window_strides=(1,),
                                   padding='VALID', dimension_numbers=('NWC', 'WIO', 'NWC'),
                                   feature_group_count=c)
    return out + b


def band_attention_prompt(q, k, v, dil, n_steps):
    B, S, H, D = q.shape
    L = S // dil
    nb = -(-L // n_steps)
    Lp = nb * n_steps
    scale = D ** -0.5

    def split(a):
        a = a.reshape(B, L, dil, H, D).transpose(0, 2, 1, 3, 4).reshape(B * dil, L, H, D)
        a = jnp.pad(a, ((0, 0), (0, Lp - L), (0, 0), (0, 0)))
        return a.reshape(B * dil, nb, n_steps, H, D)

    def with_prev(a):
        prev = jnp.pad(a[:, :-1], ((0, 0), (1, 0), (0, 0), (0, 0), (0, 0)))
        return jnp.concatenate([prev, a], axis=2)

    qb = split(q)
    kk = with_prev(split(k))
    vv = with_prev(split(v))
    s = jnp.einsum('bnqhd,bnkhd->bnhqk', qb, kk).astype(jnp.float32) * scale
    qi = jnp.arange(n_steps)[:, None]
    k_off = jnp.arange(2 * n_steps)[None, :] - n_steps
    dist = qi - k_off
    blk = jnp.arange(nb)[:, None, None]
    valid = (dist >= 0) & (dist <= n_steps) & (blk * n_steps + k_off >= 0)
    s = jnp.where(valid[None, :, None, :, :], s, NEG)
    m = jnp.max(s, axis=-1, keepdims=True)
    p = jnp.exp(s - m)
    den = jnp.sum(p, axis=-1, keepdims=True)
    o = jnp.einsum('bnhqk,bnkhd->bnqhd', (p / den).astype(v.dtype), vv)
    lse = (m + jnp.log(den))[..., 0]
    o = o.reshape(B * dil, Lp, H, D)[:, :L]
    o = o.reshape(B, dil, L, H, D).transpose(0, 2, 1, 3, 4).reshape(B, S, H, D)
    lse = lse.transpose(0, 1, 3, 2).reshape(B * dil, Lp, H)[:, :L]
    lse = lse.reshape(B, dil, L, H).transpose(0, 2, 1, 3).reshape(B, S, H)
    return o, lse


def gather_attention_sample(q, kv_buf, k_new, v_new, dil, n_steps):
    Lb = kv_buf.shape[1]
    T = q.shape[1]
    scale = q.shape[-1] ** -0.5
    kv_all = jnp.concatenate([kv_buf, jnp.stack([k_new, v_new], axis=2)], axis=1)
    idx = Lb + jnp.arange(T)[:, None] - dil * jnp.arange(n_steps + 1)[None, :]
    valid = idx >= 0
    g = kv_all[:, jnp.maximum(idx, 0)]
    s = jnp.einsum('bthd,btmhd->bhtm', q, g[:, :, :, 0]).astype(jnp.float32) * scale
    s = jnp.where(valid[None, None], s, NEG)
    m = jnp.max(s, axis=-1, keepdims=True)
    p = jnp.exp(s - m)
    den = jnp.sum(p, axis=-1, keepdims=True)
    o = jnp.einsum('bhtm,btmhd->bthd', (p / den).astype(q.dtype), g[:, :, :, 1])
    lse = (m + jnp.log(den))[..., 0].transpose(0, 2, 1)
    return o, lse, kv_all[:, T:]


def layer(x, ple, pos, kv_bufs, conv_hist, ffn_hist, prm):
    (w_in, w_attn_out, conv_dw_w, conv_dw_b, conv_ln_g, conv_ln_b, w_conv_out, w_o,
     norm_mix_pre, norm_mix_post, norm_ffn_pre, norm_ffn_post,
     w_ffn_up, ffn_dw_w, ffn_dw_b, w_ffn_down, w_ple_gate, w_ple_proj) = prm
    B, T, _ = x.shape
    h = rmsnorm(x, norm_mix_pre)
    z = h @ w_in
    a0 = 3 * ATTN_QKV
    q = rope(z[..., :ATTN_QKV].reshape(B, T, N_GROUPS * N_SLOTS, HEAD_DIM), pos)
    k = rope(z[..., ATTN_QKV:2 * ATTN_QKV].reshape(B, T, N_GROUPS * N_SLOTS, HEAD_DIM), pos)
    v = z[..., 2 * ATTN_QKV:a0].reshape(B, T, N_GROUPS * N_SLOTS, HEAD_DIM)
    q = q.reshape(B, T, N_GROUPS, N_SLOTS, HEAD_DIM)
    k = k.reshape(B, T, N_GROUPS, N_SLOTS, HEAD_DIM)
    v = v.reshape(B, T, N_GROUPS, N_SLOTS, HEAD_DIM)
    u = z[..., a0:a0 + 2 * D_CONV]
    gate_a = z[..., a0 + 2 * D_CONV:a0 + 2 * D_CONV + D_MODEL]
    gate_b = z[..., a0 + 2 * D_CONV + D_MODEL:]

    outs, lses, new_kv = [], [], []
    for gi, (win, dil) in enumerate(DIL_GROUPS):
        n_steps = win // dil
        qg, kg, vg = q[:, :, gi], k[:, :, gi], v[:, :, gi]
        if kv_bufs is None:
            o, l = band_attention_prompt(qg, kg, vg, dil, n_steps)
            keep = min(win, T)
            nk = jnp.stack([kg, vg], axis=2)[:, T - keep:]
        else:
            o, l, nk = gather_attention_sample(qg, kv_bufs[gi], kg, vg, dil, n_steps)
        outs.append(o)
        lses.append(l)
        new_kv.append(nk)
    o = jnp.stack(outs, axis=2)
    l = jnp.stack(lses, axis=2)
    wts = jax.nn.softmax(l, axis=2)
    attn = jnp.einsum('btgh,btghd->bthd', wts.astype(o.dtype), o).reshape(B, T, ATTN_OUT)
    branch_a = attn @ w_attn_out

    glu = u[..., :D_CONV] * jax.nn.sigmoid(u[..., D_CONV:])
    gh = jnp.concatenate([conv_hist, glu], axis=1)
    c = causal_dwconv(gh, conv_dw_w, conv_dw_b)
    new_conv = gh[:, gh.shape[1] - (CONV_WIDTH - 1):]
    c = layernorm(c, conv_ln_g, conv_ln_b)
    branch_b = jax.nn.silu(c) @ w_conv_out

    merged = jax.nn.sigmoid(gate_a) * branch_a + jax.nn.sigmoid(gate_b) * branch_b
    x = x + rmsnorm(merged @ w_o, norm_mix_post)

    h = rmsnorm(x, norm_ffn_pre)
    up = h @ w_ffn_up
    act_in, val = up[..., :D_FF], up[..., D_FF:]
    ah = jnp.concatenate([ffn_hist, act_in], axis=1)
    ac = causal_dwconv(ah, ffn_dw_w, ffn_dw_b)
    new_ffn = ah[:, ah.shape[1] - (FFN_CONV_WIDTH - 1):]
    f = (jax.nn.gelu(ac, approximate=True) * val) @ w_ffn_down
    x = x + rmsnorm(f, norm_ffn_post)

    x = x + jax.nn.sigmoid(x @ w_ple_gate) * (ple @ w_ple_proj)
    return x, new_kv, new_conv, new_ffn


def setup_inputs(seed: int = 0) -> dict:
    key = jax.random.key(seed)
    ks = jax.random.split(key, 32)
    f32 = jnp.float32

    def nrm(k, shape, scale):
        return jax.random.normal(k, shape, f32) * scale

    def gain(k, shape):
        return 1.0 + 0.05 * jax.random.normal(k, shape, f32)

    kv_lens = [min(w, PAST_LEN) for (w, _) in DIL_GROUPS]
    return {
        "x_prompt": nrm(ks[0], (BATCH, SEQ, D_MODEL), 1.0),
        "x_sample": nrm(ks[1], (DEC_BATCH, DEC_SEQ, D_MODEL), 1.0),
        "p_prompt": nrm(ks[2], (DEPTH, BATCH, SEQ, D_PLE), 1.0),
        "p_sample": nrm(ks[3], (DEPTH, DEC_BATCH, DEC_SEQ, D_PLE), 1.0),
        "cache_kv_w128": nrm(ks[4], (DEPTH, DEC_BATCH, kv_lens[0], 2, N_SLOTS, HEAD_DIM), 1.0),
        "cache_kv_w512": nrm(ks[5], (DEPTH, DEC_BATCH, kv_lens[1], 2, N_SLOTS, HEAD_DIM), 1.0),
        "cache_kv_w2048": nrm(ks[6], (DEPTH, DEC_BATCH, kv_lens[2], 2, N_SLOTS, HEAD_DIM), 1.0),
        "state_conv": nrm(ks[7], (DEPTH, DEC_BATCH, CONV_WIDTH - 1, D_CONV), 0.5),
        "state_ffn_conv": nrm(ks[8], (DEPTH, DEC_BATCH, FFN_CONV_WIDTH - 1, D_FF), 1.0),
        "w_in": nrm(ks[9], (DEPTH, D_MODEL, IN_COLS), D_MODEL ** -0.5),
        "w_attn_out": nrm(ks[10], (DEPTH, ATTN_OUT, D_MODEL), ATTN_OUT ** -0.5),
        "conv_dw_w": nrm(ks[11], (DEPTH, CONV_WIDTH, D_CONV), CONV_WIDTH ** -0.5),
        "conv_dw_b": nrm(ks[12], (DEPTH, D_CONV), 0.02),
        "conv_ln_g": gain(ks[13], (DEPTH, D_CONV)),
        "conv_ln_b": nrm(ks[14], (DEPTH, D_CONV), 0.02),
        "w_conv_out": nrm(ks[15], (DEPTH, D_CONV, D_MODEL), D_CONV ** -0.5),
        "w_o": nrm(ks[16], (DEPTH, D_MODEL, D_MODEL), D_MODEL ** -0.5),
        "norm_mix_pre": gain(ks[17], (DEPTH, D_MODEL)),
        "norm_mix_post": gain(ks[18], (DEPTH, D_MODEL)),
        "norm_ffn_pre": gain(ks[19], (DEPTH, D_MODEL)),
        "norm_ffn_post": gain(ks[20], (DEPTH, D_MODEL)),
        "w_ffn_up": nrm(ks[21], (DEPTH, D_MODEL, 2 * D_FF), D_MODEL ** -0.5),
        "ffn_dw_w": nrm(ks[22], (DEPTH, FFN_CONV_WIDTH, D_FF), FFN_CONV_WIDTH ** -0.5),
        "ffn_dw_b": nrm(ks[23], (DEPTH, D_FF), 0.02),
        "w_ffn_down": nrm(ks[24], (DEPTH, D_FF, D_MODEL), D_FF ** -0.5),
        "w_ple_gate": nrm(ks[25], (DEPTH, D_MODEL, D_MODEL), D_MODEL ** -0.5),
        "w_ple_proj": nrm(ks[26], (DEPTH, D_PLE, D_MODEL), D_PLE ** -0.5),
    }


def reference(x_prompt, x_sample, p_prompt, p_sample, cache_kv_w128, cache_kv_w512, cache_kv_w2048,
              state_conv, state_ffn_conv, w_in, w_attn_out, conv_dw_w, conv_dw_b, conv_ln_g, conv_ln_b,
              w_conv_out, w_o, norm_mix_pre, norm_mix_post, norm_ffn_pre, norm_ffn_post,
              w_ffn_up, ffn_dw_w, ffn_dw_b, w_ffn_down, w_ple_gate, w_ple_proj):
    pos_p = jnp.arange(x_prompt.shape[1], dtype=jnp.int32)
    pos_s = PAST_LEN + jnp.arange(x_sample.shape[1], dtype=jnp.int32)
    xp, xs = x_prompt, x_sample
    bp = x_prompt.shape[0]
    kvp = ([], [], [])
    kvs = ([], [], [])
    conv_p, conv_s, ffn_p, ffn_s = [], [], [], []
    for i in range(DEPTH):
        prm = (w_in[i], w_attn_out[i], conv_dw_w[i], conv_dw_b[i], conv_ln_g[i], conv_ln_b[i],
               w_conv_out[i], w_o[i], norm_mix_pre[i], norm_mix_post[i], norm_ffn_pre[i],
               norm_ffn_post[i], w_ffn_up[i], ffn_dw_w[i], ffn_dw_b[i], w_ffn_down[i],
               w_ple_gate[i], w_ple_proj[i])
        zc = jnp.zeros((bp, CONV_WIDTH - 1, D_CONV), xp.dtype)
        zf = jnp.zeros((bp, FFN_CONV_WIDTH - 1, D_FF), xp.dtype)
        xp, nkv, nc, nf = layer(xp, p_prompt[i], pos_p, None, zc, zf, prm)
        for g in range(N_GROUPS):
            kvp[g].append(nkv[g])
        conv_p.append(nc)
        ffn_p.append(nf)
        bufs = (cache_kv_w128[i], cache_kv_w512[i], cache_kv_w2048[i])
        xs, nkv, nc, nf = layer(xs, p_sample[i], pos_s, bufs, state_conv[i], state_ffn_conv[i], prm)
        for g in range(N_GROUPS):
            kvs[g].append(nkv[g])
        conv_s.append(nc)
        ffn_s.append(nf)
    return (xp, xs,
            jnp.stack(kvp[0]), jnp.stack(kvp[1]), jnp.stack(kvp[2]), jnp.stack(conv_p), jnp.stack(ffn_p),
            jnp.stack(kvs[0]), jnp.stack(kvs[1]), jnp.stack(kvs[2]), jnp.stack(conv_s), jnp.stack(ffn_s))
```

```python
import functools

import jax
import jax.numpy as jnp
from jax import lax
from jax.experimental import pallas as pl
from jax.experimental.pallas import tpu as pltpu

HEAD_DIM = 128
N_SLOTS = 8
DIL_GROUPS = ((128, 1), (512, 4), (2048, 16))
N_GROUPS = len(DIL_GROUPS)
GROUP_COLS = N_SLOTS * HEAD_DIM
N_STEPS = 128
CONV_WIDTH = 31
FFN_CONV_WIDTH = 3
ROPE_THETA = 10000.0
EPS = 1e-6
NEG = -1e30

LANE = 128
SUBLANE = 8
VMEM_LIMIT = 56 * 1024 * 1024

BF16 = jnp.bfloat16
F32 = jnp.float32


def _params(*sem):
    return pltpu.CompilerParams(dimension_semantics=sem, vmem_limit_bytes=VMEM_LIMIT)


def _sigmoid(x):
    return 1.0 / (1.0 + jnp.exp(-x))


def _rms(x, g):
    return x * lax.rsqrt(jnp.mean(x * x, axis=-1, keepdims=True) + EPS) * g


def _dot(a, b):
    return jnp.dot(a, b, preferred_element_type=F32)


def _in_proj_kernel(*refs, epilogue, scale):
    x_ref, g_ref = refs[0], refs[1]
    o_ref, h_ref = refs[-2], refs[-1]

    @pl.when(pl.program_id(1) == 0)
    def _():
        h_ref[...] = _rms(x_ref[...], g_ref[...]).astype(BF16)

    h = h_ref[...]
    if epilogue == "glu":
        za = _dot(h, refs[2][...])
        zb = _dot(h, refs[3][...])
        o_ref[...] = (za * _sigmoid(zb)).astype(o_ref.dtype)
    elif epilogue == "rope":
        z = _dot(h, refs[2][...])
        cos = refs[3][...]
        sin = refs[4][...]
        for hh in range(z.shape[1] // HEAD_DIM):
            sl = slice(hh * HEAD_DIM, (hh + 1) * HEAD_DIM)
            zc = z[:, sl]
            rot = pltpu.roll(zc, HEAD_DIM // 2, axis=1)
            o_ref[:, sl] = ((zc * cos + rot * sin) * scale).astype(o_ref.dtype)
    elif epilogue == "sigmoid":
        o_ref[...] = _sigmoid(_dot(h, refs[2][...])).astype(o_ref.dtype)
    else:
        o_ref[...] = _dot(h, refs[2][...]).astype(o_ref.dtype)


def _in_proj(x, gain, w_in, layer, col0, ncols, *, epilogue, out_dtype, rope=None, scale=1.0, tm=512, tn=1024):
    m, dm = x.shape
    cb0 = col0 // tn
    in_specs = [pl.BlockSpec((tm, dm), lambda i, j: (i, 0)), pl.BlockSpec((1, dm), lambda i, j: (0, 0))]
    args = [x, gain]
    in_specs.append(pl.BlockSpec((None, dm, tn), lambda i, j: (layer, 0, cb0 + j)))
    args.append(w_in)
    if epilogue == "glu":
        in_specs.append(pl.BlockSpec((None, dm, tn), lambda i, j: (layer, 0, cb0 + ncols // tn + j)))
        args.append(w_in)
    if epilogue == "rope":
        for tab in rope:
            in_specs.append(pl.BlockSpec((tm, HEAD_DIM), lambda i, j: (i, 0)))
            args.append(tab)
    return pl.pallas_call(
        functools.partial(_in_proj_kernel, epilogue=epilogue, scale=scale),
        grid=(m // tm, ncols // tn),
        in_specs=in_specs,
        out_specs=pl.BlockSpec((tm, tn), lambda i, j: (i, j)),
        out_shape=jax.ShapeDtypeStruct((m, ncols), out_dtype),
        scratch_shapes=[pltpu.VMEM((tm, dm), BF16)],
        compiler_params=_params("parallel", "arbitrary"),
    )(*args)


def _attn_prompt_kernel(q_ref, kp_ref, kc_ref, vp_ref, vc_ref, o_ref, lse_ref, *, nb):
    lb = pl.program_id(0) % nb
    qi = lax.broadcasted_iota(jnp.int32, (N_STEPS, N_STEPS), 0)
    kj = lax.broadcasted_iota(jnp.int32, (N_STEPS, N_STEPS), 1)
    own_ok = kj <= qi
    prev_ok = jnp.logical_and(kj >= qi, lb > 0)
    lane = lax.broadcasted_iota(jnp.int32, (N_STEPS, LANE), 1)
    lse_tile = jnp.zeros((N_STEPS, LANE), F32)
    dn = (((1,), (1,)), ((), ()))
    for h in range(N_SLOTS):
        sl = slice(h * HEAD_DIM, (h + 1) * HEAD_DIM)
        q = q_ref[:, sl]
        s_own = lax.dot_general(q, kc_ref[:, sl].astype(BF16), dn, preferred_element_type=F32)
        s_prev = lax.dot_general(q, kp_ref[:, sl].astype(BF16), dn, preferred_element_type=F32)
        s_own = jnp.where(own_ok, s_own, NEG)
        s_prev = jnp.where(prev_ok, s_prev, NEG)
        m = jnp.maximum(jnp.max(s_own, axis=-1, keepdims=True), jnp.max(s_prev, axis=-1, keepdims=True))
        p_own = jnp.exp(s_own - m)
        p_prev = jnp.exp(s_prev - m)
        den = jnp.sum(p_own, axis=-1, keepdims=True) + jnp.sum(p_prev, axis=-1, keepdims=True)
        inv = 1.0 / den
        o = _dot((p_own * inv).astype(BF16), vc_ref[:, sl].astype(BF16))
        o = o + _dot((p_prev * inv).astype(BF16), vp_ref[:, sl].astype(BF16))
        o_ref[:, sl] = o
        lse_tile = jnp.where(lane == h, m + jnp.log(den), lse_tile)
    lse_ref[...] = lse_tile


def _attn_prompt(q, k, v, gi, n_batch):
    m, cols = q.shape
    _, dil = DIL_GROUPS[gi]
    nb = m // n_batch // dil // N_STEPS
    gpr = cols // GROUP_COLS
    rows = m // dil
    q2, k2, v2 = (a.reshape(rows, dil * cols) for a in (q, k, v))

    def cur(i, r):
        return (i, r * gpr + gi)

    def prev(i, r):
        return (jnp.maximum(i - 1, 0), r * gpr + gi)

    blk = (N_STEPS, GROUP_COLS)
    o, lse = pl.pallas_call(
        functools.partial(_attn_prompt_kernel, nb=nb),
        grid=(n_batch * nb, dil),
        in_specs=[pl.BlockSpec(blk, cur), pl.BlockSpec(blk, prev), pl.BlockSpec(blk, cur),
                  pl.BlockSpec(blk, prev), pl.BlockSpec(blk, cur)],
        out_specs=[pl.BlockSpec(blk, lambda i, r: (i, r)), pl.BlockSpec((N_STEPS, LANE), lambda i, r: (i, r))],
        out_shape=[jax.ShapeDtypeStruct((rows, dil * GROUP_COLS), F32),
                   jax.ShapeDtypeStruct((rows, dil * LANE), F32)],
        compiler_params=_params("parallel", "parallel"),
    )(q2, k2, k2, v2, v2)
    return o.reshape(m, GROUP_COLS), lse.reshape(m, LANE)


def _split_dot(p, e):
    hi = p.astype(BF16)
    lo = (p - hi.astype(F32)).astype(BF16)
    return _dot(hi, e) + _dot(lo, e)


def _attn_sample_kernel(q_ref, kn_ref, vn_ref, buf_ref, e_ref, et_ref, o_ref, lse_ref, *, dil, n_new):
    e = e_ref[...]
    et = et_ref[...]
    row = lax.broadcasted_iota(jnp.int32, (N_STEPS, LANE), 0)
    new_row = lax.broadcasted_iota(jnp.int32, (SUBLANE, LANE), 0)
    new_row_w = lax.broadcasted_iota(jnp.int32, (SUBLANE, GROUP_COLS), 0)
    k_new = jnp.zeros((SUBLANE, GROUP_COLS), F32)
    v_new = jnp.zeros((SUBLANE, GROUP_COLS), F32)
    for s in range(n_new):
        k_new = jnp.where(new_row_w == s, kn_ref[s], k_new)
        v_new = jnp.where(new_row_w == s, vn_ref[s], v_new)
    row_w = 2 * GROUP_COLS
    for t in range(n_new):
        koff = 0 if dil == 1 else t * row_w
        qt = q_ref[t]
        s_buf = _split_dot(buf_ref[:, koff:koff + GROUP_COLS] * qt, e)
        s_new = _split_dot(k_new * qt, e)
        if dil == 1:
            s_buf = jnp.where(row >= t, s_buf, NEG)
            s_new = jnp.where(new_row <= t, s_new, NEG)
        else:
            s_new = jnp.where(new_row == t, s_new, NEG)
        m = jnp.maximum(jnp.max(s_buf, axis=0, keepdims=True), jnp.max(s_new, axis=0, keepdims=True))
        p_buf = jnp.exp(s_buf - m)
        p_new = jnp.exp(s_new - m)
        den = jnp.sum(p_buf, axis=0, keepdims=True) + jnp.sum(p_new, axis=0, keepdims=True)
        inv = 1.0 / den
        w_buf = _split_dot(p_buf * inv, et)
        w_new = _split_dot(p_new * inv, et)
        o = jnp.sum(w_buf * buf_ref[:, koff + GROUP_COLS:koff + row_w], axis=0, keepdims=True)
        o_ref[t] = o + jnp.sum(w_new * v_new, axis=0, keepdims=True)
        lse_ref[t] = m + jnp.log(den)


def _attn_sample(q, k, v, cache, layer, gi, head_sum, head_spread):
    _, dil = DIL_GROUPS[gi]
    depth, nseq, lb = cache.shape[:3]
    n_new = q.shape[0] // nseq
    assert lb == dil * N_STEPS and (dil == 1 or n_new <= dil) and n_new <= SUBLANE
    row_w = 2 * GROUP_COLS
    buf = cache.reshape(depth, nseq, N_STEPS, dil * row_w)
    buf_w = row_w if dil == 1 else n_new * row_w
    q4, k4, v4 = (a.reshape(n_new, nseq, 1, a.shape[-1]) for a in (q, k, v))
    new_spec = pl.BlockSpec((n_new, None, 1, GROUP_COLS), lambda b: (0, b, 0, gi))
    o, lse = pl.pallas_call(
        functools.partial(_attn_sample_kernel, dil=dil, n_new=n_new),
        grid=(nseq,),
        in_specs=[new_spec, new_spec, new_spec,
                  pl.BlockSpec((None, None, N_STEPS, buf_w), lambda b: (layer, b, 0, 0)),
                  pl.BlockSpec((GROUP_COLS, LANE), lambda b: (0, 0)),
                  pl.BlockSpec((LANE, GROUP_COLS), lambda b: (0, 0))],
        out_specs=[pl.BlockSpec((n_new, None, 1, GROUP_COLS), lambda b: (0, b, 0, 0)),
                   pl.BlockSpec((n_new, None, 1, LANE), lambda b: (0, b, 0, 0))],
        out_shape=[jax.ShapeDtypeStruct((n_new, nseq, 1, GROUP_COLS), F32),
                   jax.ShapeDtypeStruct((n_new, nseq, 1, LANE), F32)],
        compiler_params=_params("parallel"),
    )(q4, k4, v4, buf, head_sum, head_spread)
    return o.reshape(n_new * nseq, GROUP_COLS), lse.reshape(n_new * nseq, LANE)


def _dwconv_kernel(x_ref, hist_ref, w_ref, b_ref, o_ref, ext_ref, *, width, shift, bps, chunk):
    tm = x_ref.shape[0]
    hist = hist_ref.shape[0]
    base = hist - (width - 1) * shift

    @pl.when(pl.program_id(1) % bps == 0)
    def _():
        ext_ref[0:hist, :] = hist_ref[...]

    ext_ref[hist:hist + tm, :] = x_ref[...]
    bias = b_ref[...]
    for r in range(0, tm, chunk):
        acc = jnp.broadcast_to(bias, (chunk, bias.shape[1]))
        if shift == 1:
            for res in range(SUBLANE):
                taps = list(range(res, width, SUBLANE))
                lo = base + r + res
                win = ext_ref[lo:lo + chunk + SUBLANE * (len(taps) - 1), :]
                for a, j in enumerate(taps):
                    acc = acc + w_ref[j:j + 1, :] * win[SUBLANE * a:SUBLANE * a + chunk]
        else:
            for j in range(width):
                lo = base + r + j * shift
                acc = acc + w_ref[j:j + 1, :] * ext_ref[lo:lo + chunk, :]
        o_ref[r:r + chunk, :] = acc
    if bps > 1:
        ext_ref[0:hist, :] = ext_ref[tm:tm + hist, :]


def _dwconv(x, hist, w, b, *, shift, tm=512, tc=256, chunk=32):
    m, c = x.shape
    nseq, hrows, _ = hist.shape
    width = w.shape[0]
    bps = m // nseq // tm
    return pl.pallas_call(
        functools.partial(_dwconv_kernel, width=width, shift=shift, bps=bps, chunk=chunk),
        grid=(c // tc, m // tm),
        in_specs=[pl.BlockSpec((tm, tc), lambda ci, i: (i, ci)),
                  pl.BlockSpec((None, hrows, tc), lambda ci, i: (i // bps, 0, ci)),
                  pl.BlockSpec((width, tc), lambda ci, i: (0, ci)),
                  pl.BlockSpec((1, tc), lambda ci, i: (0, ci))],
        out_specs=pl.BlockSpec((tm, tc), lambda ci, i: (i, ci)),
        out_shape=jax.ShapeDtypeStruct((m, c), F32),
        scratch_shapes=[pltpu.VMEM((hrows + tm, tc), F32)],
        compiler_params=_params("parallel", "arbitrary"),
    )(x, hist, w, b)


def _merge_kernel(o0_ref, o1_ref, o2_ref, l0_ref, l1_ref, l2_ref, c_ref, ga_ref, gb_ref, x_ref,
                  wa_ref, wc_ref, wo_ref, lng_ref, lnb_ref, npost_ref, nffn_ref,
                  xo_ref, h2_ref, attn_ref):
    o_refs = (o0_ref, o1_ref, o2_ref)
    lses = [r[...] for r in (l0_ref, l1_ref, l2_ref)]
    mx = jnp.maximum(jnp.maximum(lses[0], lses[1]), lses[2])
    es = [jnp.exp(l - mx) for l in lses]
    inv = 1.0 / (es[0] + es[1] + es[2])
    wts = [e * inv for e in es]
    for h in range(N_SLOTS):
        sl = slice(h * HEAD_DIM, (h + 1) * HEAD_DIM)
        a = wts[0][:, h:h + 1] * o_refs[0][:, sl]
        a = a + wts[1][:, h:h + 1] * o_refs[1][:, sl]
        a = a + wts[2][:, h:h + 1] * o_refs[2][:, sl]
        attn_ref[:, sl] = a.astype(BF16)
    branch_a = _dot(attn_ref[...], wa_ref[...])

    c = c_ref[...]
    mu = jnp.mean(c, axis=-1, keepdims=True)
    cc = c - mu
    var = jnp.mean(cc * cc, axis=-1, keepdims=True)
    y = cc * lax.rsqrt(var + EPS) * lng_ref[...] + lnb_ref[...]
    branch_b = _dot((y * _sigmoid(y)).astype(BF16), wc_ref[...])

    merged = ga_ref[...].astype(F32) * branch_a + gb_ref[...].astype(F32) * branch_b
    mix = _dot(merged.astype(BF16), wo_ref[...])
    x_new = x_ref[...] + _rms(mix, npost_ref[...])
    xo_ref[...] = x_new
    h2_ref[...] = _rms(x_new, nffn_ref[...]).astype(BF16)


def _merge(os_, lses, c, gates, x, w_attn_out, w_conv_out, w_o, ln_g, ln_b, n_post, n_ffn, *, tm=256):
    m, dm = x.shape
    dc = c.shape[1]

    def row(width):
        return pl.BlockSpec((tm, width), lambda i: (i, 0))

    def const(shape):
        return pl.BlockSpec(shape, lambda i: (0,) * len(shape), pipeline_mode=pl.Buffered(1))

    in_specs = ([row(GROUP_COLS)] * 3 + [row(LANE)] * 3 + [row(dc), row(dm), pl.BlockSpec((tm, dm), lambda i: (i, 1)),
                row(dm), const(w_attn_out.shape), const(w_conv_out.shape), const(w_o.shape),
                const((1, dc)), const((1, dc)), const((1, dm)), const((1, dm))])
    return pl.pallas_call(
        _merge_kernel,
        grid=(m // tm,),
        in_specs=in_specs,
        out_specs=[row(dm), row(dm)],
        out_shape=[jax.ShapeDtypeStruct((m, dm), F32), jax.ShapeDtypeStruct((m, dm), BF16)],
        scratch_shapes=[pltpu.VMEM((tm, GROUP_COLS), BF16)],
        compiler_params=_params("parallel"),
    )(*os_, *lses, c, gates, gates, x, w_attn_out, w_conv_out, w_o, ln_g, ln_b, n_post, n_ffn)


def _gelu_tanh(x):
    return 0.5 * x * (1.0 + jnp.tanh(0.7978845608028654 * (x + 0.044715 * (x * x * x))))


def _ffn_kernel(*refs, shift, bps):
    (h_ref, x_ref, hist_ref, wa_ref, wv_ref, dw_ref, db_ref, wd_ref, g_ref, xo_ref, tail_ref, acc_ref, ext_ref) = refs[:13]
    carry_ref = refs[13] if bps > 1 else None
    i, j = pl.program_id(0), pl.program_id(1)
    tm = h_ref.shape[0]
    hist = hist_ref.shape[0]
    h = h_ref[...]
    a = _dot(h, wa_ref[...])
    val = _dot(h, wv_ref[...])
    if bps > 1:
        first = i % bps == 0

        @pl.when(first)
        def _():
            ext_ref[0:hist, :] = hist_ref[...]

        @pl.when(jnp.logical_not(first))
        def _():
            ext_ref[0:hist, :] = carry_ref[j]
    else:
        ext_ref[0:hist, :] = hist_ref[...]
    ext_ref[hist:hist + tm, :] = a
    ac = db_ref[...] + dw_ref[2:3, :] * a
    for tap in range(FFN_CONV_WIDTH - 1):
        lo = hist - (FFN_CONV_WIDTH - 1 - tap) * shift
        ac = ac + dw_ref[tap:tap + 1, :] * ext_ref[lo:lo + tm, :]
    tail = ext_ref[tm:tm + hist, :]
    tail_ref[...] = tail
    if bps > 1:
        carry_ref[j] = tail
    contrib = _dot((_gelu_tanh(ac) * val).astype(BF16), wd_ref[...])

    @pl.when(j == 0)
    def _():
        acc_ref[...] = contrib

    @pl.when(j > 0)
    def _():
        acc_ref[...] += contrib

    @pl.when(j == pl.num_programs(1) - 1)
    def _():
        xo_ref[...] = x_ref[...] + _rms(acc_ref[...], g_ref[...])


def _ffn(h2, x, hist, w_up, dw_w, dw_b, w_down, n_post, layer, *, shift, tm=512, tf=512):
    m, dm = x.shape
    nseq, hrows, dff = hist.shape
    bps = m // nseq // tm
    nf = dff // tf
    scratch = [pltpu.VMEM((tm, dm), F32), pltpu.VMEM((hrows + tm, tf), F32)]
    if bps > 1:
        scratch.append(pltpu.VMEM((nf, hrows, tf), F32))
    return pl.pallas_call(
        functools.partial(_ffn_kernel, shift=shift, bps=bps),
        grid=(m // tm, nf),
        in_specs=[pl.BlockSpec((tm, dm), lambda i, j: (i, 0)),
                  pl.BlockSpec((tm, dm), lambda i, j: (i, 0)),
                  pl.BlockSpec((None, hrows, tf), lambda i, j: (i // bps, 0, j)),
                  pl.BlockSpec((None, dm, tf), lambda i, j: (layer, 0, j)),
                  pl.BlockSpec((None, dm, tf), lambda i, j: (layer, 0, nf + j)),
                  pl.BlockSpec((FFN_CONV_WIDTH, tf), lambda i, j: (0, j)),
                  pl.BlockSpec((1, tf), lambda i, j: (0, j)),
                  pl.BlockSpec((None, tf, dm), lambda i, j: (layer, j, 0)),
                  pl.BlockSpec((1, dm), lambda i, j: (0, 0))],
        out_specs=[pl.BlockSpec((tm, dm), lambda i, j: (i, 0)),
                   pl.BlockSpec((None, hrows, tf), lambda i, j: (i, 0, j))],
        out_shape=[jax.ShapeDtypeStruct((m, dm), F32), jax.ShapeDtypeStruct((m // tm, hrows, dff), F32)],
        scratch_shapes=scratch,
        compiler_params=_params("arbitrary", "arbitrary"),
    )(h2, x, hist, w_up, w_up, dw_w, dw_b, w_down, n_post)


def _ple_kernel(x_ref, p_ref, wg_ref, wp_ref, o_ref):
    x = x_ref[...]
    gate = _sigmoid(_dot(x.astype(BF16), wg_ref[...]))
    o_ref[...] = x + gate * _dot(p_ref[...].astype(BF16), wp_ref[...])


def _ple(x, p, w_gate, w_proj, *, tm=512):
    m, dm = x.shape
    dp = p.shape[1]
    return pl.pallas_call(
        _ple_kernel,
        grid=(m // tm,),
        in_specs=[pl.BlockSpec((tm, dm), lambda i: (i, 0)), pl.BlockSpec((tm, dp), lambda i: (i, 0)),
                  pl.BlockSpec((dm, dm), lambda i: (0, 0), pipeline_mode=pl.Buffered(1)),
                  pl.BlockSpec((dp, dm), lambda i: (0, 0), pipeline_mode=pl.Buffered(1))],
        out_specs=pl.BlockSpec((tm, dm), lambda i: (i, 0)),
        out_shape=jax.ShapeDtypeStruct((m, dm), F32),
        compiler_params=_params("parallel"),
    )(x, p, w_gate, w_proj)


def _rope_tables(pos):
    half = HEAD_DIM // 2
    inv = ROPE_THETA ** (-jnp.arange(half, dtype=F32) / half)
    ang = pos.astype(F32)[:, None] * inv[None, :]
    cos, sin = jnp.cos(ang), jnp.sin(ang)
    return jnp.concatenate([cos, cos], axis=-1), jnp.concatenate([-sin, sin], axis=-1)


def _layer(x, ple, rope, conv_hist, ffn_hist, wts, layer, *, shift, attend, q_dtype):
    qkv = N_GROUPS * GROUP_COLS
    dconv = wts["conv_dw_w"].shape[-1]
    dm = x.shape[1]
    gain = wts["norm_mix_pre"][layer][None]
    proj = functools.partial(_in_proj, x, gain, wts["w_in"], layer)
    q = proj(0, qkv, epilogue="rope", out_dtype=q_dtype, rope=rope, scale=HEAD_DIM ** -0.5)
    k = proj(qkv, qkv, epilogue="rope", out_dtype=F32, rope=rope)
    v = proj(2 * qkv, qkv, epilogue="plain", out_dtype=F32)
    glu = proj(3 * qkv, dconv, epilogue="glu", out_dtype=F32)
    gates = proj(3 * qkv + 2 * dconv, 2 * dm, epilogue="sigmoid", out_dtype=BF16)

    os_, lses = zip(*(attend(q, k, v, gi) for gi in range(N_GROUPS)))
    c = _dwconv(glu, conv_hist, wts["conv_dw_w"][layer], wts["conv_dw_b"][layer][None], shift=shift)
    x, h2 = _merge(os_, lses, c, gates, x, wts["w_attn_out"][layer], wts["w_conv_out"][layer], wts["w_o"][layer],
                   wts["conv_ln_g"][layer][None], wts["conv_ln_b"][layer][None],
                   wts["norm_mix_post"][layer][None], wts["norm_ffn_pre"][layer][None])
    x, ffn_tail = _ffn(h2, x, ffn_hist, wts["w_ffn_up"], wts["ffn_dw_w"][layer], wts["ffn_dw_b"][layer][None],
                       wts["w_ffn_down"], wts["norm_ffn_post"][layer][None], layer, shift=shift)
    x = _ple(x, ple, wts["w_ple_gate"][layer], wts["w_ple_proj"][layer])
    return x, k, v, glu, ffn_tail


def kernel(x_prompt, x_sample, p_prompt, p_sample, cache_kv_w128, cache_kv_w512, cache_kv_w2048, state_conv, state_ffn_conv, w_in, w_attn_out, conv_dw_w, conv_dw_b, conv_ln_g, conv_ln_b, w_conv_out, w_o, norm_mix_pre, norm_mix_post, norm_ffn_pre, norm_ffn_post, w_ffn_up, ffn_dw_w, ffn_dw_b, w_ffn_down, w_ple_gate, w_ple_proj):
    depth = w_in.shape[0]
    nb_p, seq, dm = x_prompt.shape
    nb_s, t_new, _ = x_sample.shape
    caches = (cache_kv_w128, cache_kv_w512, cache_kv_w2048)
    past = cache_kv_w2048.shape[2]
    dconv = conv_dw_w.shape[-1]
    dff = ffn_dw_w.shape[-1]
    conv_hist_rows = -(-(CONV_WIDTH - 1) // SUBLANE) * SUBLANE
    wts = dict(
        w_in=w_in.astype(BF16), w_attn_out=w_attn_out.astype(BF16), w_conv_out=w_conv_out.astype(BF16),
        w_o=w_o.astype(BF16), w_ffn_up=w_ffn_up.astype(BF16), w_ffn_down=w_ffn_down.astype(BF16),
        w_ple_gate=w_ple_gate.astype(BF16), w_ple_proj=w_ple_proj.astype(BF16),
        conv_dw_w=conv_dw_w, conv_dw_b=conv_dw_b, conv_ln_g=conv_ln_g, conv_ln_b=conv_ln_b,
        norm_mix_pre=norm_mix_pre, norm_mix_post=norm_mix_post, norm_ffn_pre=norm_ffn_pre,
        norm_ffn_post=norm_ffn_post, ffn_dw_w=ffn_dw_w, ffn_dw_b=ffn_dw_b)

    rope_p = _rope_tables(jnp.tile(jnp.arange(seq, dtype=jnp.int32), nb_p))
    rope_s = _rope_tables(jnp.repeat(past + jnp.arange(t_new, dtype=jnp.int32), nb_s))
    lane_head = jnp.arange(GROUP_COLS, dtype=jnp.int32)[:, None] // HEAD_DIM
    head_sum = (lane_head == jnp.arange(LANE, dtype=jnp.int32)[None, :]).astype(BF16)
    head_spread = head_sum.T

    xp = x_prompt.reshape(nb_p * seq, dm)
    xs = x_sample.transpose(1, 0, 2).reshape(t_new * nb_s, dm)
    zero_conv = jnp.zeros((nb_p, conv_hist_rows, dconv), F32)
    zero_ffn = jnp.zeros((nb_p, SUBLANE, dff), F32)

    kv_p = [[] for _ in DIL_GROUPS]
    kv_s = [[] for _ in DIL_GROUPS]
    conv_p, conv_s, ffn_p, ffn_s = [], [], [], []
    for i in range(depth):
        attend_p = lambda q, k, v, gi: _attn_prompt(q, k, v, gi, nb_p)
        xp, k, v, glu, tail = _layer(xp, p_prompt[i].reshape(nb_p * seq, -1), rope_p, zero_conv, zero_ffn, wts, i,
                                     shift=1, attend=attend_p, q_dtype=BF16)
        k5 = k.reshape(nb_p, seq, N_GROUPS, N_SLOTS, HEAD_DIM)
        v5 = v.reshape(nb_p, seq, N_GROUPS, N_SLOTS, HEAD_DIM)
        for gi, (win, _) in enumerate(DIL_GROUPS):
            keep = min(win, seq)
            kv_p[gi].append(jnp.stack([k5[:, seq - keep:, gi], v5[:, seq - keep:, gi]], axis=2))
        conv_p.append(glu.reshape(nb_p, seq, dconv)[:, seq - (CONV_WIDTH - 1):])
        ffn_p.append(tail.reshape(nb_p, -1, SUBLANE, dff)[:, -1, SUBLANE - (FFN_CONV_WIDTH - 1):])

        attend_s = lambda q, k, v, gi, i=i: _attn_sample(q, k, v, caches[gi], i, gi, head_sum, head_spread)
        conv_hist = state_conv[i].transpose(1, 0, 2).reshape(1, (CONV_WIDTH - 1) * nb_s, dconv)
        ffn_hist = state_ffn_conv[i].transpose(1, 0, 2).reshape(1, (FFN_CONV_WIDTH - 1) * nb_s, dff)
        xs, k, v, glu, tail = _layer(xs, p_sample[i].transpose(1, 0, 2).reshape(t_new * nb_s, -1), rope_s,
                                     conv_hist, ffn_hist, wts, i, shift=nb_s, attend=attend_s, q_dtype=F32)
        k5 = k.reshape(t_new, nb_s, N_GROUPS, N_SLOTS, HEAD_DIM)
        v5 = v.reshape(t_new, nb_s, N_GROUPS, N_SLOTS, HEAD_DIM)
        for gi in range(N_GROUPS):
            kv_s[gi].append(jnp.stack([k5[:, :, gi], v5[:, :, gi]], axis=2).transpose(1, 0, 2, 3, 4))
        glu_bt = glu.reshape(t_new, nb_s, dconv).transpose(1, 0, 2)
        conv_s.append(jnp.concatenate([state_conv[i][:, t_new:], glu_bt], axis=1))
        ffn_s.append(tail.reshape(FFN_CONV_WIDTH - 1, nb_s, dff).transpose(1, 0, 2))

    kv_s_out = [jnp.concatenate([caches[gi][:, :, t_new:], jnp.stack(kv_s[gi])], axis=2) for gi in range(N_GROUPS)]
    return (xp.reshape(nb_p, seq, dm), xs.reshape(t_new, nb_s, dm).transpose(1, 0, 2),
            jnp.stack(kv_p[0]), jnp.stack(kv_p[1]), jnp.stack(kv_p[2]), jnp.stack(conv_p), jnp.stack(ffn_p),
            kv_s_out[0], kv_s_out[1], kv_s_out[2], jnp.stack(conv_s), jnp.stack(ffn_s))
```

```python
import functools

import jax
import jax.numpy as jnp
from jax import lax
from jax.experimental import pallas as pl
from jax.experimental.pallas import tpu as pltpu

HEAD_DIM = 128
N_SLOTS = 8
DIL_GROUPS = ((128, 1), (512, 4), (2048, 16))
N_GROUPS = len(DIL_GROUPS)
GROUP_COLS = N_SLOTS * HEAD_DIM
N_STEPS = 128
CONV_WIDTH = 31
FFN_CONV_WIDTH = 3
ROPE_THETA = 10000.0
EPS = 1e-6
NEG = -1e30

LANE = 128
SUBLANE = 8
VMEM_LIMIT = 56 * 1024 * 1024

BF16 = jnp.bfloat16
F32 = jnp.float32


def _params(*sem):
    return pltpu.CompilerParams(dimension_semantics=sem, vmem_limit_bytes=VMEM_LIMIT)


def _sigmoid(x):
    return 1.0 / (1.0 + jnp.exp(-x))


def _rms(x, g):
    return x * lax.rsqrt(jnp.mean(x * x, axis=-1, keepdims=True) + EPS) * g


def _dot(a, b):
    return jnp.dot(a, b, preferred_element_type=F32)


def _in_proj_kernel(*refs, epilogue, scale):
    x_ref, g_ref = refs[0], refs[1]
    o_ref, h_ref = refs[-2], refs[-1]

    @pl.when(pl.program_id(1) == 0)
    def _():
        h_ref[...] = _rms(x_ref[...], g_ref[...]).astype(BF16)

    h = h_ref[...]
    if epilogue == "glu":
        za = _dot(h, refs[2][...])
        zb = _dot(h, refs[3][...])
        o_ref[...] = (za * _sigmoid(zb)).astype(o_ref.dtype)
    elif epilogue == "rope":
        z = _dot(h, refs[2][...])
        cos = refs[3][...]
        sin = refs[4][...]
        for hh in range(z.shape[1] // HEAD_DIM):
            sl = slice(hh * HEAD_DIM, (hh + 1) * HEAD_DIM)
            zc = z[:, sl]
            rot = pltpu.roll(zc, HEAD_DIM // 2, axis=1)
            o_ref[:, sl] = ((zc * cos + rot * sin) * scale).astype(o_ref.dtype)
    elif epilogue == "sigmoid":
        o_ref[...] = _sigmoid(_dot(h, refs[2][...])).astype(o_ref.dtype)
    else:
        o_ref[...] = _dot(h, refs[2][...]).astype(o_ref.dtype)


def _in_proj(x, gain, w_in, layer, col0, ncols, *, epilogue, out_dtype, rope=None, scale=1.0, tm=512, tn=1024):
    m, dm = x.shape
    cb0 = col0 // tn
    in_specs = [pl.BlockSpec((tm, dm), lambda i, j: (i, 0)), pl.BlockSpec((1, dm), lambda i, j: (0, 0))]
    args = [x, gain]
    in_specs.append(pl.BlockSpec((None, dm, tn), lambda i, j: (layer, 0, cb0 + j)))
    args.append(w_in)
    if epilogue == "glu":
        in_specs.append(pl.BlockSpec((None, dm, tn), lambda i, j: (layer, 0, cb0 + ncols // tn + j)))
        args.append(w_in)
    if epilogue == "rope":
        for tab in rope:
            in_specs.append(pl.BlockSpec((tm, HEAD_DIM), lambda i, j: (i, 0)))
            args.append(tab)
    return pl.pallas_call(
        functools.partial(_in_proj_kernel, epilogue=epilogue, scale=scale),
        grid=(m // tm, ncols // tn),
        in_specs=in_specs,
        out_specs=pl.BlockSpec((tm, tn), lambda i, j: (i, j)),
        out_shape=jax.ShapeDtypeStruct((m, ncols), out_dtype),
        scratch_shapes=[pltpu.VMEM((tm, dm), BF16)],
        compiler_params=_params("parallel", "arbitrary"),
        name=f"in_proj_{epilogue}",
    )(*args)


def _attn_prompt_kernel(q_ref, kp_ref, kc_ref, vp_ref, vc_ref, o_ref, lse_ref, *, dil, nb, hp):
    first = pl.program_id(0) % nb == 0
    qi = lax.broadcasted_iota(jnp.int32, (N_STEPS, 2 * N_STEPS), 0)
    kj = lax.broadcasted_iota(jnp.int32, (N_STEPS, 2 * N_STEPS), 1)
    prev_ok = jnp.logical_and(jnp.logical_and(kj >= qi, kj < N_STEPS), jnp.logical_not(first))
    valid = jnp.logical_or(prev_ok, jnp.logical_and(kj >= N_STEPS, kj - N_STEPS <= qi))
    lane = lax.broadcasted_iota(jnp.int32, (N_STEPS, LANE), 1)
    dn = (((1,), (1,)), ((), ()))
    hg = pl.program_id(1)

    @pl.when(hg == 0)
    def _():
        lse_ref[...] = jnp.zeros(lse_ref.shape, F32)

    for r in range(dil):
        cls = pl.ds(r, N_STEPS, stride=dil) if dil > 1 else slice(None)
        lse_tile = lse_ref[cls, :]
        for h in range(hp):
            sl = slice(h * HEAD_DIM, (h + 1) * HEAD_DIM)
            q = q_ref[cls, sl].astype(BF16)
            keys = jnp.concatenate([kp_ref[cls, sl], kc_ref[cls, sl]], axis=0).astype(BF16)
            vals = jnp.concatenate([vp_ref[cls, sl], vc_ref[cls, sl]], axis=0).astype(BF16)
            s = jnp.where(valid, lax.dot_general(q, keys, dn, preferred_element_type=F32), NEG)
            m = jnp.max(s, axis=-1, keepdims=True)
            p = jnp.exp(s - m)
            den = jnp.sum(p, axis=-1, keepdims=True)
            o_ref[cls, sl] = _dot((p * (1.0 / den)).astype(BF16), vals)
            lse_tile = jnp.where(lane == hg * hp + h, m + jnp.log(den), lse_tile)
        lse_ref[cls, :] = lse_tile


def _attn_prompt(q, k, v, gi, n_batch):
    m, cols = q.shape
    _, dil = DIL_GROUPS[gi]
    hp = N_SLOTS if dil == 1 else 1
    nhg = N_SLOTS // hp
    rows = N_STEPS * dil
    nb = m // n_batch // rows

    def cur(i, hg):
        return (i, gi * nhg + hg)

    def prev(i, hg):
        return (jnp.maximum(i - 1, 0), gi * nhg + hg)

    blk = (rows, hp * HEAD_DIM)
    o, lse = pl.pallas_call(
        functools.partial(_attn_prompt_kernel, dil=dil, nb=nb, hp=hp),
        grid=(m // rows, nhg),
        in_specs=[pl.BlockSpec(blk, cur), pl.BlockSpec(blk, prev), pl.BlockSpec(blk, cur),
                  pl.BlockSpec(blk, prev), pl.BlockSpec(blk, cur)],
        out_specs=[pl.BlockSpec(blk, lambda i, hg: (i, hg)), pl.BlockSpec((rows, LANE), lambda i, hg: (i, 0))],
        out_shape=[jax.ShapeDtypeStruct((m, GROUP_COLS), F32), jax.ShapeDtypeStruct((m, LANE), F32)],
        compiler_params=_params("parallel", "arbitrary"),
        name=f"attn_prompt_d{dil}",
    )(q, k, k, v, v)
    return o, lse


def _lane_sum(p, ones):
    hi = p.astype(BF16)
    lo = (p - hi.astype(F32)).astype(BF16)
    return _dot(hi, ones) + _dot(lo, ones)


def _attn_sample_kernel(q_ref, kn_ref, vn_ref, buf_ref, ones_ref, o_ref, lse_ref, s_ref, *, dil, n_new, chunk):
    ones = ones_ref[...]
    kn = kn_ref[...]
    vn = vn_ref[...]
    new_idx = lax.broadcasted_iota(jnp.int32, (n_new, N_SLOTS, HEAD_DIM), 0)
    row_idx = lax.broadcasted_iota(jnp.int32, (chunk, N_SLOTS, HEAD_DIM), 0)

    def scores(keys, qt):
        n = keys.shape[0]
        return _lane_sum((keys * qt).reshape(n * N_SLOTS, HEAD_DIM), ones).reshape(n, N_SLOTS, HEAD_DIM)

    for t in range(n_new):
        cls = 0 if dil == 1 else t
        qt = q_ref[t]
        s_new = jnp.where(new_idx <= t if dil == 1 else new_idx == t, scores(kn, qt), NEG)
        m = jnp.max(s_new, axis=0)
        for r0 in range(0, N_STEPS, chunk):
            s = scores(buf_ref[r0:r0 + chunk, cls, 0], qt)
            if dil == 1 and r0 < n_new:
                s = jnp.where(row_idx + r0 >= t, s, NEG)
            s_ref[r0:r0 + chunk] = s
            m = jnp.maximum(m, jnp.max(s, axis=0))
        p_new = jnp.exp(s_new - m)
        den = jnp.sum(p_new, axis=0)
        acc = jnp.sum(p_new * vn, axis=0)
        for r0 in range(0, N_STEPS, chunk):
            p = jnp.exp(s_ref[r0:r0 + chunk] - m)
            den = den + jnp.sum(p, axis=0)
            acc = acc + jnp.sum(p * buf_ref[r0:r0 + chunk, cls, 1], axis=0)
        o_ref[t] = acc / den
        lse_ref[t] = m + jnp.log(den)


def _attn_sample(q, k, v, cache, layer, gi, ones, *, chunk=16):
    _, dil = DIL_GROUPS[gi]
    depth, nseq, lb = cache.shape[:3]
    n_new = q.shape[0] // nseq
    assert lb == dil * N_STEPS and (dil == 1 or n_new <= dil) and n_new <= chunk
    ncls = 1 if dil == 1 else n_new
    buf = cache.reshape(depth, nseq, N_STEPS, dil, 2, N_SLOTS, HEAD_DIM)
    q5, k5, v5 = (a.reshape(n_new, nseq, N_GROUPS, N_SLOTS, HEAD_DIM) for a in (q, k, v))
    new_spec = pl.BlockSpec((n_new, None, None, N_SLOTS, HEAD_DIM), lambda b: (0, b, gi, 0, 0))
    out_spec = pl.BlockSpec((n_new, None, N_SLOTS, HEAD_DIM), lambda b: (0, b, 0, 0))
    out_shape = jax.ShapeDtypeStruct((n_new, nseq, N_SLOTS, HEAD_DIM), F32)
    o, lse = pl.pallas_call(
        functools.partial(_attn_sample_kernel, dil=dil, n_new=n_new, chunk=chunk),
        grid=(nseq,),
        in_specs=[new_spec, new_spec, new_spec,
                  pl.BlockSpec((None, None, N_STEPS, ncls, 2, N_SLOTS, HEAD_DIM),
                               lambda b: (layer, b, 0, 0, 0, 0, 0)),
                  pl.BlockSpec((LANE, LANE), lambda b: (0, 0))],
        out_specs=[out_spec, out_spec],
        out_shape=[out_shape, out_shape],
        scratch_shapes=[pltpu.VMEM((N_STEPS, N_SLOTS, HEAD_DIM), F32)],
        compiler_params=_params("parallel"),
        name=f"attn_sample_d{dil}",
    )(q5, k5, v5, buf, ones)
    lse = jnp.pad(lse[..., 0], ((0, 0), (0, 0), (0, LANE - N_SLOTS)))
    return o.reshape(n_new * nseq, GROUP_COLS), lse.reshape(n_new * nseq, LANE)


def _cache_shift_kernel(*refs, n_buf, n_chunks):
    srcs, news, outs, sem = refs[:n_buf], refs[n_buf:2 * n_buf], refs[2 * n_buf:3 * n_buf], refs[3 * n_buf]
    copies = []
    for src, new, out in zip(srcs, news, outs):
        depth, nseq, lb = src.shape[:3]
        t = new.shape[2]
        per = nseq // n_chunks
        for layer in range(depth):
            for c in range(n_chunks):
                seqs = pl.ds(c * per, per)
                copies.append(pltpu.make_async_copy(src.at[layer, seqs, pl.ds(t, lb - t)],
                                                    out.at[layer, seqs, pl.ds(0, lb - t)], sem.at[len(copies)]))
        copies.append(pltpu.make_async_copy(new, out.at[:, :, pl.ds(lb - t, t)], sem.at[len(copies)]))
    for cp in copies:
        cp.start()
    for cp in copies:
        cp.wait()


def _cache_shift(caches, news, *, n_chunks=4):
    n_buf = len(caches)
    n_copies = sum(c.shape[0] * n_chunks + 1 for c in caches)
    any_spec = pl.BlockSpec(memory_space=pl.ANY)
    return pl.pallas_call(
        functools.partial(_cache_shift_kernel, n_buf=n_buf, n_chunks=n_chunks),
        in_specs=[any_spec] * (2 * n_buf),
        out_specs=[any_spec] * n_buf,
        out_shape=[jax.ShapeDtypeStruct(c.shape, c.dtype) for c in caches],
        scratch_shapes=[pltpu.SemaphoreType.DMA((n_copies,))],
        name="cache_shift",
    )(*caches, *news)


def _dwconv_kernel(x_ref, hist_ref, w_ref, b_ref, o_ref, ext_ref, *, width, shift, bps, chunk):
    tm = x_ref.shape[0]
    hist = hist_ref.shape[0]
    base = hist - (width - 1) * shift

    @pl.when(pl.program_id(1) % bps == 0)
    def _():
        ext_ref[0:hist, :] = hist_ref[...]

    ext_ref[hist:hist + tm, :] = x_ref[...]
    bias = b_ref[...]
    for r in range(0, tm, chunk):
        acc = jnp.broadcast_to(bias, (chunk, bias.shape[1]))
        if shift == 1:
            for res in range(SUBLANE):
                taps = list(range(res, width, SUBLANE))
                lo = base + r + res
                win = ext_ref[lo:lo + chunk + SUBLANE * (len(taps) - 1), :]
                for a, j in enumerate(taps):
                    acc = acc + w_ref[j:j + 1, :] * win[SUBLANE * a:SUBLANE * a + chunk]
        else:
            for j in range(width):
                lo = base + r + j * shift
                acc = acc + w_ref[j:j + 1, :] * ext_ref[lo:lo + chunk, :]
        o_ref[r:r + chunk, :] = acc
    if bps > 1:
        ext_ref[0:hist, :] = ext_ref[tm:tm + hist, :]


def _dwconv(x, hist, w, b, *, shift, tm=512, tc=256, chunk=32):
    m, c = x.shape
    nseq, hrows, _ = hist.shape
    width = w.shape[0]
    bps = m // nseq // tm
    return pl.pallas_call(
        functools.partial(_dwconv_kernel, width=width, shift=shift, bps=bps, chunk=chunk),
        grid=(c // tc, m // tm),
        in_specs=[pl.BlockSpec((tm, tc), lambda ci, i: (i, ci)),
                  pl.BlockSpec((None, hrows, tc), lambda ci, i: (i // bps, 0, ci)),
                  pl.BlockSpec((width, tc), lambda ci, i: (0, ci)),
                  pl.BlockSpec((1, tc), lambda ci, i: (0, ci))],
        out_specs=pl.BlockSpec((tm, tc), lambda ci, i: (i, ci)),
        out_shape=jax.ShapeDtypeStruct((m, c), F32),
        scratch_shapes=[pltpu.VMEM((hrows + tm, tc), F32)],
        compiler_params=_params("parallel", "arbitrary"),
        name="dwconv",
    )(x, hist, w, b)


def _merge_kernel(o0_ref, o1_ref, o2_ref, l0_ref, l1_ref, l2_ref, c_ref, ga_ref, gb_ref, x_ref,
                  wa_ref, wc_ref, wo_ref, lng_ref, lnb_ref, npost_ref, nffn_ref,
                  xo_ref, h2_ref, attn_ref):
    o_refs = (o0_ref, o1_ref, o2_ref)
    lses = [r[...] for r in (l0_ref, l1_ref, l2_ref)]
    mx = jnp.maximum(jnp.maximum(lses[0], lses[1]), lses[2])
    es = [jnp.exp(l - mx) for l in lses]
    inv = 1.0 / (es[0] + es[1] + es[2])
    wts = [e * inv for e in es]
    for h in range(N_SLOTS):
        sl = slice(h * HEAD_DIM, (h + 1) * HEAD_DIM)
        a = wts[0][:, h:h + 1] * o_refs[0][:, sl]
        a = a + wts[1][:, h:h + 1] * o_refs[1][:, sl]
        a = a + wts[2][:, h:h + 1] * o_refs[2][:, sl]
        attn_ref[:, sl] = a.astype(BF16)
    branch_a = _dot(attn_ref[...], wa_ref[...])

    c = c_ref[...]
    mu = jnp.mean(c, axis=-1, keepdims=True)
    cc = c - mu
    var = jnp.mean(cc * cc, axis=-1, keepdims=True)
    y = cc * lax.rsqrt(var + EPS) * lng_ref[...] + lnb_ref[...]
    branch_b = _dot((y * _sigmoid(y)).astype(BF16), wc_ref[...])

    merged = ga_ref[...].astype(F32) * branch_a + gb_ref[...].astype(F32) * branch_b
    mix = _dot(merged.astype(BF16), wo_ref[...])
    x_new = x_ref[...] + _rms(mix, npost_ref[...])
    xo_ref[...] = x_new
    h2_ref[...] = _rms(x_new, nffn_ref[...]).astype(BF16)


def _merge(os_, lses, c, gates, x, w_attn_out, w_conv_out, w_o, ln_g, ln_b, n_post, n_ffn, *, tm=256):
    m, dm = x.shape
    dc = c.shape[1]

    def row(width):
        return pl.BlockSpec((tm, width), lambda i: (i, 0))

    def const(shape):
        return pl.BlockSpec(shape, lambda i: (0,) * len(shape), pipeline_mode=pl.Buffered(1))

    in_specs = ([row(GROUP_COLS)] * 3 + [row(LANE)] * 3
                + [row(dc), row(dm), pl.BlockSpec((tm, dm), lambda i: (i, 1)),
                   row(dm), const(w_attn_out.shape), const(w_conv_out.shape), const(w_o.shape),
                   const((1, dc)), const((1, dc)), const((1, dm)), const((1, dm))])
    return pl.pallas_call(
        _merge_kernel,
        name="merge",
        grid=(m // tm,),
        in_specs=in_specs,
        out_specs=[row(dm), row(dm)],
        out_shape=[jax.ShapeDtypeStruct((m, dm), F32), jax.ShapeDtypeStruct((m, dm), BF16)],
        scratch_shapes=[pltpu.VMEM((tm, GROUP_COLS), BF16)],
        compiler_params=_params("parallel"),
    )(*os_, *lses, c, gates, gates, x, w_attn_out, w_conv_out, w_o, ln_g, ln_b, n_post, n_ffn)


def _gelu_tanh(x):
    return 0.5 * x * (1.0 + jnp.tanh(0.7978845608028654 * (x + 0.044715 * (x * x * x))))


def _ffn_kernel(*refs, shift, bps):
    (h_ref, x_ref, hist_ref, wa_ref, wv_ref, dw_ref, db_ref, wd_ref, g_ref, xo_ref, tail_ref, acc_ref, ext_ref) = refs[:13]
    carry_ref = refs[13] if bps > 1 else None
    i, j = pl.program_id(0), pl.program_id(1)
    tm = h_ref.shape[0]
    hist = hist_ref.shape[0]
    h = h_ref[...]
    a = _dot(h, wa_ref[...])
    val = _dot(h, wv_ref[...])
    if bps > 1:
        first = i % bps == 0

        @pl.when(first)
        def _():
            ext_ref[0:hist, :] = hist_ref[...]

        @pl.when(jnp.logical_not(first))
        def _():
            ext_ref[0:hist, :] = carry_ref[j]
    else:
        ext_ref[0:hist, :] = hist_ref[...]
    ext_ref[hist:hist + tm, :] = a
    ac = db_ref[...] + dw_ref[2:3, :] * a
    for tap in range(FFN_CONV_WIDTH - 1):
        lo = hist - (FFN_CONV_WIDTH - 1 - tap) * shift
        ac = ac + dw_ref[tap:tap + 1, :] * ext_ref[lo:lo + tm, :]
    tail = ext_ref[tm:tm + hist, :]
    tail_ref[...] = tail
    if bps > 1:
        carry_ref[j] = tail
    contrib = _dot((_gelu_tanh(ac) * val).astype(BF16), wd_ref[...])

    @pl.when(j == 0)
    def _():
        acc_ref[...] = contrib

    @pl.when(j > 0)
    def _():
        acc_ref[...] += contrib

    @pl.when(j == pl.num_programs(1) - 1)
    def _():
        xo_ref[...] = x_ref[...] + _rms(acc_ref[...], g_ref[...])


def _ffn(h2, x, hist, w_up, dw_w, dw_b, w_down, n_post, layer, *, shift, tm=512, tf=512):
    m, dm = x.shape
    nseq, hrows, dff = hist.shape
    bps = m // nseq // tm
    nf = dff // tf
    scratch = [pltpu.VMEM((tm, dm), F32), pltpu.VMEM((hrows + tm, tf), F32)]
    if bps > 1:
        scratch.append(pltpu.VMEM((nf, hrows, tf), F32))
    return pl.pallas_call(
        functools.partial(_ffn_kernel, shift=shift, bps=bps),
        grid=(m // tm, nf),
        in_specs=[pl.BlockSpec((tm, dm), lambda i, j: (i, 0)),
                  pl.BlockSpec((tm, dm), lambda i, j: (i, 0)),
                  pl.BlockSpec((None, hrows, tf), lambda i, j: (i // bps, 0, j)),
                  pl.BlockSpec((None, dm, tf), lambda i, j: (layer, 0, j)),
                  pl.BlockSpec((None, dm, tf), lambda i, j: (layer, 0, nf + j)),
                  pl.BlockSpec((FFN_CONV_WIDTH, tf), lambda i, j: (0, j)),
                  pl.BlockSpec((1, tf), lambda i, j: (0, j)),
                  pl.BlockSpec((None, tf, dm), lambda i, j: (layer, j, 0)),
                  pl.BlockSpec((1, dm), lambda i, j: (0, 0))],
        out_specs=[pl.BlockSpec((tm, dm), lambda i, j: (i, 0)),
                   pl.BlockSpec((None, hrows, tf), lambda i, j: (i, 0, j))],
        out_shape=[jax.ShapeDtypeStruct((m, dm), F32), jax.ShapeDtypeStruct((m // tm, hrows, dff), F32)],
        scratch_shapes=scratch,
        name="ffn",
        compiler_params=_params("arbitrary", "arbitrary"),
    )(h2, x, hist, w_up, w_up, dw_w, dw_b, w_down, n_post)


def _ple_kernel(x_ref, p_ref, wg_ref, wp_ref, o_ref):
    x = x_ref[...]
    gate = _sigmoid(_dot(x.astype(BF16), wg_ref[...]))
    o_ref[...] = x + gate * _dot(p_ref[...].astype(BF16), wp_ref[...])


def _ple(x, p, w_gate, w_proj, *, tm=512):
    m, dm = x.shape
    dp = p.shape[1]
    return pl.pallas_call(
        _ple_kernel,
        grid=(m // tm,),
        in_specs=[pl.BlockSpec((tm, dm), lambda i: (i, 0)), pl.BlockSpec((tm, dp), lambda i: (i, 0)),
                  pl.BlockSpec((dm, dm), lambda i: (0, 0), pipeline_mode=pl.Buffered(1)),
                  pl.BlockSpec((dp, dm), lambda i: (0, 0), pipeline_mode=pl.Buffered(1))],
        out_specs=pl.BlockSpec((tm, dm), lambda i: (i, 0)),
        out_shape=jax.ShapeDtypeStruct((m, dm), F32),
        compiler_params=_params("parallel"),
        name="ple",
    )(x, p, w_gate, w_proj)


def _rope_tables(pos):
    half = HEAD_DIM // 2
    inv = ROPE_THETA ** (-jnp.arange(half, dtype=F32) / half)
    ang = pos.astype(F32)[:, None] * inv[None, :]
    cos, sin = jnp.cos(ang), jnp.sin(ang)
    return jnp.concatenate([cos, cos], axis=-1), jnp.concatenate([-sin, sin], axis=-1)


def _layer(x, ple, rope, conv_hist, ffn_hist, wts, layer, *, shift, attend):
    qkv = N_GROUPS * GROUP_COLS
    dconv = wts["conv_dw_w"].shape[-1]
    dm = x.shape[1]
    gain = wts["norm_mix_pre"][layer][None]
    proj = functools.partial(_in_proj, x, gain, wts["w_in"], layer)
    q = proj(0, qkv, epilogue="rope", out_dtype=F32, rope=rope, scale=HEAD_DIM ** -0.5)
    k = proj(qkv, qkv, epilogue="rope", out_dtype=F32, rope=rope)
    v = proj(2 * qkv, qkv, epilogue="plain", out_dtype=F32)
    glu = proj(3 * qkv, dconv, epilogue="glu", out_dtype=F32)
    gates = proj(3 * qkv + 2 * dconv, 2 * dm, epilogue="sigmoid", out_dtype=BF16)

    os_, lses = zip(*(attend(q, k, v, gi) for gi in range(N_GROUPS)))
    c = _dwconv(glu, conv_hist, wts["conv_dw_w"][layer], wts["conv_dw_b"][layer][None], shift=shift)
    x, h2 = _merge(os_, lses, c, gates, x, wts["w_attn_out"][layer], wts["w_conv_out"][layer], wts["w_o"][layer],
                   wts["conv_ln_g"][layer][None], wts["conv_ln_b"][layer][None],
                   wts["norm_mix_post"][layer][None], wts["norm_ffn_pre"][layer][None])
    x, ffn_tail = _ffn(h2, x, ffn_hist, wts["w_ffn_up"], wts["ffn_dw_w"][layer], wts["ffn_dw_b"][layer][None],
                       wts["w_ffn_down"], wts["norm_ffn_post"][layer][None], layer, shift=shift)
    x = _ple(x, ple, wts["w_ple_gate"][layer], wts["w_ple_proj"][layer])
    return x, k, v, glu, ffn_tail


def kernel(x_prompt, x_sample, p_prompt, p_sample, cache_kv_w128, cache_kv_w512, cache_kv_w2048, state_conv, state_ffn_conv, w_in, w_attn_out, conv_dw_w, conv_dw_b, conv_ln_g, conv_ln_b, w_conv_out, w_o, norm_mix_pre, norm_mix_post, norm_ffn_pre, norm_ffn_post, w_ffn_up, ffn_dw_w, ffn_dw_b, w_ffn_down, w_ple_gate, w_ple_proj):
    depth = w_in.shape[0]
    nb_p, seq, dm = x_prompt.shape
    nb_s, t_new, _ = x_sample.shape
    caches = (cache_kv_w128, cache_kv_w512, cache_kv_w2048)
    past = cache_kv_w2048.shape[2]
    dconv = conv_dw_w.shape[-1]
    dff = ffn_dw_w.shape[-1]
    conv_hist_rows = -(-(CONV_WIDTH - 1) // SUBLANE) * SUBLANE
    wts = dict(
        w_in=w_in.astype(BF16), w_attn_out=w_attn_out.astype(BF16), w_conv_out=w_conv_out.astype(BF16),
        w_o=w_o.astype(BF16), w_ffn_up=w_ffn_up.astype(BF16), w_ffn_down=w_ffn_down.astype(BF16),
        w_ple_gate=w_ple_gate.astype(BF16), w_ple_proj=w_ple_proj.astype(BF16),
        conv_dw_w=conv_dw_w, conv_dw_b=conv_dw_b, conv_ln_g=conv_ln_g, conv_ln_b=conv_ln_b,
        norm_mix_pre=norm_mix_pre, norm_mix_post=norm_mix_post, norm_ffn_pre=norm_ffn_pre,
        norm_ffn_post=norm_ffn_post, ffn_dw_w=ffn_dw_w, ffn_dw_b=ffn_dw_b)

    rope_p = _rope_tables(jnp.tile(jnp.arange(seq, dtype=jnp.int32), nb_p))
    rope_s = _rope_tables(jnp.repeat(past + jnp.arange(t_new, dtype=jnp.int32), nb_s))
    ones = jnp.ones((LANE, LANE), BF16)

    xp = x_prompt.reshape(nb_p * seq, dm)
    xs = x_sample.transpose(1, 0, 2).reshape(t_new * nb_s, dm)
    zero_conv = jnp.zeros((nb_p, conv_hist_rows, dconv), F32)
    zero_ffn = jnp.zeros((nb_p, SUBLANE, dff), F32)

    kv_p = [[] for _ in DIL_GROUPS]
    kv_s = [[] for _ in DIL_GROUPS]
    conv_p, conv_s, ffn_p, ffn_s = [], [], [], []
    for i in range(depth):
        attend_p = lambda q, k, v, gi: _attn_prompt(q, k, v, gi, nb_p)
        xp, k, v, glu, tail = _layer(xp, p_prompt[i].reshape(nb_p * seq, -1), rope_p, zero_conv, zero_ffn, wts, i,
                                     shift=1, attend=attend_p)
        k5 = k.reshape(nb_p, seq, N_GROUPS, N_SLOTS, HEAD_DIM)
        v5 = v.reshape(nb_p, seq, N_GROUPS, N_SLOTS, HEAD_DIM)
        for gi, (win, _) in enumerate(DIL_GROUPS):
            keep = min(win, seq)
            kv_p[gi].append(jnp.stack([k5[:, seq - keep:, gi], v5[:, seq - keep:, gi]], axis=2))
        conv_p.append(glu.reshape(nb_p, seq, dconv)[:, seq - (CONV_WIDTH - 1):])
        ffn_p.append(tail.reshape(nb_p, -1, SUBLANE, dff)[:, -1, SUBLANE - (FFN_CONV_WIDTH - 1):])

        attend_s = lambda q, k, v, gi, i=i: _attn_sample(q, k, v, caches[gi], i, gi, ones)
        conv_hist = state_conv[i].transpose(1, 0, 2).reshape(1, (CONV_WIDTH - 1) * nb_s, dconv)
        ffn_hist = state_ffn_conv[i].transpose(1, 0, 2).reshape(1, (FFN_CONV_WIDTH - 1) * nb_s, dff)
        xs, k, v, glu, tail = _layer(xs, p_sample[i].transpose(1, 0, 2).reshape(t_new * nb_s, -1), rope_s,
                                     conv_hist, ffn_hist, wts, i, shift=nb_s, attend=attend_s)
        k5 = k.reshape(t_new, nb_s, N_GROUPS, N_SLOTS, HEAD_DIM)
        v5 = v.reshape(t_new, nb_s, N_GROUPS, N_SLOTS, HEAD_DIM)
        for gi in range(N_GROUPS):
            kv_s[gi].append(jnp.stack([k5[:, :, gi], v5[:, :, gi]], axis=2).transpose(1, 0, 2, 3, 4))
        glu_bt = glu.reshape(t_new, nb_s, dconv).transpose(1, 0, 2)
        conv_s.append(jnp.concatenate([state_conv[i][:, t_new:], glu_bt], axis=1))
        ffn_s.append(tail.reshape(FFN_CONV_WIDTH - 1, nb_s, dff).transpose(1, 0, 2))

    kv_s_out = _cache_shift(caches, [jnp.stack(rows) for rows in kv_s])
    return (xp.reshape(nb_p, seq, dm), xs.reshape(t_new, nb_s, dm).transpose(1, 0, 2),
            jnp.stack(kv_p[0]), jnp.stack(kv_p[1]), jnp.stack(kv_p[2]), jnp.stack(conv_p), jnp.stack(ffn_p),
            kv_s_out[0], kv_s_out[1], kv_s_out[2], jnp.stack(conv_s), jnp.stack(ffn_s))
```

```python
import functools

import jax
import jax.numpy as jnp
from jax import lax
from jax.experimental import pallas as pl
from jax.experimental.pallas import tpu as pltpu

HEAD_DIM = 128
N_SLOTS = 8
DIL_GROUPS = ((128, 1), (512, 4), (2048, 16))
N_GROUPS = len(DIL_GROUPS)
GROUP_COLS = N_SLOTS * HEAD_DIM
N_STEPS = 128
CONV_WIDTH = 31
FFN_CONV_WIDTH = 3
ROPE_THETA = 10000.0
EPS = 1e-6
NEG = -1e30

LANE = 128
SUBLANE = 8
VMEM_LIMIT = 56 * 1024 * 1024

BF16 = jnp.bfloat16
F32 = jnp.float32


def _params(*sem):
    return pltpu.CompilerParams(dimension_semantics=sem, vmem_limit_bytes=VMEM_LIMIT)


def _sigmoid(x):
    return 1.0 / (1.0 + jnp.exp(-x))


def _rms(x, g):
    return x * lax.rsqrt(jnp.mean(x * x, axis=-1, keepdims=True) + EPS) * g


def _dot(a, b):
    return jnp.dot(a, b, preferred_element_type=F32)


def _in_proj_kernel(*refs, epilogue, scale):
    x_ref, g_ref = refs[0], refs[1]
    o_ref, h_ref = refs[-2], refs[-1]

    @pl.when(pl.program_id(1) == 0)
    def _():
        h_ref[...] = _rms(x_ref[...], g_ref[...]).astype(BF16)

    h = h_ref[...]
    if epilogue == "glu":
        za = _dot(h, refs[2][...])
        zb = _dot(h, refs[3][...])
        o_ref[...] = (za * _sigmoid(zb)).astype(o_ref.dtype)
    elif epilogue == "rope":
        z = _dot(h, refs[2][...])
        cos = refs[3][...]
        sin = refs[4][...]
        for hh in range(z.shape[1] // HEAD_DIM):
            sl = slice(hh * HEAD_DIM, (hh + 1) * HEAD_DIM)
            zc = z[:, sl]
            rot = pltpu.roll(zc, HEAD_DIM // 2, axis=1)
            o_ref[:, sl] = ((zc * cos + rot * sin) * scale).astype(o_ref.dtype)
    elif epilogue == "sigmoid":
        o_ref[...] = _sigmoid(_dot(h, refs[2][...])).astype(o_ref.dtype)
    else:
        o_ref[...] = _dot(h, refs[2][...]).astype(o_ref.dtype)


def _in_proj(x, gain, w_in, layer, col0, ncols, *, epilogue, out_dtype, rope=None, scale=1.0, tm=512, tn=1024):
    m, dm = x.shape
    cb0 = col0 // tn
    in_specs = [pl.BlockSpec((tm, dm), lambda i, j: (i, 0)), pl.BlockSpec((1, dm), lambda i, j: (0, 0))]
    args = [x, gain]
    in_specs.append(pl.BlockSpec((None, dm, tn), lambda i, j: (layer, 0, cb0 + j)))
    args.append(w_in)
    if epilogue == "glu":
        in_specs.append(pl.BlockSpec((None, dm, tn), lambda i, j: (layer, 0, cb0 + ncols // tn + j)))
        args.append(w_in)
    if epilogue == "rope":
        for tab in rope:
            in_specs.append(pl.BlockSpec((tm, HEAD_DIM), lambda i, j: (i, 0)))
            args.append(tab)
    return pl.pallas_call(
        functools.partial(_in_proj_kernel, epilogue=epilogue, scale=scale),
        grid=(m // tm, ncols // tn),
        in_specs=in_specs,
        out_specs=pl.BlockSpec((tm, tn), lambda i, j: (i, j)),
        out_shape=jax.ShapeDtypeStruct((m, ncols), out_dtype),
        scratch_shapes=[pltpu.VMEM((tm, dm), BF16)],
        compiler_params=_params("parallel", "arbitrary"),
        name=f"in_proj_{epilogue}",
    )(*args)


def _attn_prompt_kernel(q_ref, kp_ref, kc_ref, vp_ref, vc_ref, o_ref, lse_ref, *, dil, nb, hp):
    first = pl.program_id(0) % nb == 0
    qi = lax.broadcasted_iota(jnp.int32, (N_STEPS, 2 * N_STEPS), 0)
    kj = lax.broadcasted_iota(jnp.int32, (N_STEPS, 2 * N_STEPS), 1)
    prev_ok = jnp.logical_and(jnp.logical_and(kj >= qi, kj < N_STEPS), jnp.logical_not(first))
    valid = jnp.logical_or(prev_ok, jnp.logical_and(kj >= N_STEPS, kj - N_STEPS <= qi))
    lane = lax.broadcasted_iota(jnp.int32, (N_STEPS, LANE), 1)
    dn = (((1,), (1,)), ((), ()))
    hg = pl.program_id(1)

    @pl.when(hg == 0)
    def _():
        lse_ref[...] = jnp.zeros(lse_ref.shape, F32)

    for r in range(dil):
        cls = pl.ds(r, N_STEPS, stride=dil) if dil > 1 else slice(None)
        lse_tile = lse_ref[cls, :]
        for h in range(hp):
            sl = slice(h * HEAD_DIM, (h + 1) * HEAD_DIM)
            q = q_ref[cls, sl].astype(BF16)
            keys = jnp.concatenate([kp_ref[cls, sl], kc_ref[cls, sl]], axis=0).astype(BF16)
            vals = jnp.concatenate([vp_ref[cls, sl], vc_ref[cls, sl]], axis=0).astype(BF16)
            s = jnp.where(valid, lax.dot_general(q, keys, dn, preferred_element_type=F32), NEG)
            m = jnp.max(s, axis=-1, keepdims=True)
            p = jnp.exp(s - m)
            den = jnp.sum(p, axis=-1, keepdims=True)
            o_ref[cls, sl] = _dot((p * (1.0 / den)).astype(BF16), vals)
            lse_tile = jnp.where(lane == hg * hp + h, m + jnp.log(den), lse_tile)
        lse_ref[cls, :] = lse_tile


def _attn_prompt(q, k, v, gi, n_batch):
    m, cols = q.shape
    _, dil = DIL_GROUPS[gi]
    hp = N_SLOTS if dil == 1 else 1
    nhg = N_SLOTS // hp
    rows = N_STEPS * dil
    nb = m // n_batch // rows

    def cur(i, hg):
        return (i, gi * nhg + hg)

    def prev(i, hg):
        return (jnp.maximum(i - 1, 0), gi * nhg + hg)

    blk = (rows, hp * HEAD_DIM)
    o, lse = pl.pallas_call(
        functools.partial(_attn_prompt_kernel, dil=dil, nb=nb, hp=hp),
        grid=(m // rows, nhg),
        in_specs=[pl.BlockSpec(blk, cur), pl.BlockSpec(blk, prev), pl.BlockSpec(blk, cur),
                  pl.BlockSpec(blk, prev), pl.BlockSpec(blk, cur)],
        out_specs=[pl.BlockSpec(blk, lambda i, hg: (i, hg)), pl.BlockSpec((rows, LANE), lambda i, hg: (i, 0))],
        out_shape=[jax.ShapeDtypeStruct((m, GROUP_COLS), F32), jax.ShapeDtypeStruct((m, LANE), F32)],
        compiler_params=_params("parallel", "arbitrary"),
        name=f"attn_prompt_d{dil}",
    )(q, k, k, v, v)
    return o, lse


def _lane_sum(p, ones):
    hi = p.astype(BF16)
    lo = (p - hi.astype(F32)).astype(BF16)
    return _dot(hi, ones) + _dot(lo, ones)


def _attn_sample_kernel(q_ref, kn_ref, vn_ref, buf_ref, ones_ref, o_ref, lse_ref, s_ref, *, dil, n_new, chunk):
    ones = ones_ref[...]
    kn = kn_ref[...]
    vn = vn_ref[...]
    new_idx = lax.broadcasted_iota(jnp.int32, (n_new, N_SLOTS, HEAD_DIM), 0)
    row_idx = lax.broadcasted_iota(jnp.int32, (chunk, N_SLOTS, HEAD_DIM), 0)

    def scores(keys, qt):
        n = keys.shape[0]
        return _lane_sum((keys * qt).reshape(n * N_SLOTS, HEAD_DIM), ones).reshape(n, N_SLOTS, HEAD_DIM)

    for t in range(n_new):
        cls = 0 if dil == 1 else t
        qt = q_ref[t]
        s_new = jnp.where(new_idx <= t if dil == 1 else new_idx == t, scores(kn, qt), NEG)
        m = jnp.max(s_new, axis=0)
        for r0 in range(0, N_STEPS, chunk):
            s = scores(buf_ref[r0:r0 + chunk, cls, 0], qt)
            if dil == 1 and r0 < n_new:
                s = jnp.where(row_idx + r0 >= t, s, NEG)
            s_ref[r0:r0 + chunk] = s
            m = jnp.maximum(m, jnp.max(s, axis=0))
        p_new = jnp.exp(s_new - m)
        den = jnp.sum(p_new, axis=0)
        acc = jnp.sum(p_new * vn, axis=0)
        for r0 in range(0, N_STEPS, chunk):
            p = jnp.exp(s_ref[r0:r0 + chunk] - m)
            den = den + jnp.sum(p, axis=0)
            acc = acc + jnp.sum(p * buf_ref[r0:r0 + chunk, cls, 1], axis=0)
        o_ref[t] = acc / den
        lse_ref[t] = m + jnp.log(den)


def _attn_sample(q, k, v, cache, layer, gi, ones, *, chunk=16):
    _, dil = DIL_GROUPS[gi]
    depth, nseq, lb = cache.shape[:3]
    n_new = q.shape[0] // nseq
    assert lb == dil * N_STEPS and (dil == 1 or n_new <= dil) and n_new <= chunk
    ncls = 1 if dil == 1 else n_new
    buf = cache.reshape(depth, nseq, N_STEPS, dil, 2, N_SLOTS, HEAD_DIM)
    q5, k5, v5 = (a.reshape(n_new, nseq, N_GROUPS, N_SLOTS, HEAD_DIM) for a in (q, k, v))
    new_spec = pl.BlockSpec((n_new, None, None, N_SLOTS, HEAD_DIM), lambda b: (0, b, gi, 0, 0))
    out_spec = pl.BlockSpec((n_new, None, N_SLOTS, HEAD_DIM), lambda b: (0, b, 0, 0))
    out_shape = jax.ShapeDtypeStruct((n_new, nseq, N_SLOTS, HEAD_DIM), F32)
    o, lse = pl.pallas_call(
        functools.partial(_attn_sample_kernel, dil=dil, n_new=n_new, chunk=chunk),
        grid=(nseq,),
        in_specs=[new_spec, new_spec, new_spec,
                  pl.BlockSpec((None, None, N_STEPS, ncls, 2, N_SLOTS, HEAD_DIM),
                               lambda b: (layer, b, 0, 0, 0, 0, 0)),
                  pl.BlockSpec((LANE, LANE), lambda b: (0, 0))],
        out_specs=[out_spec, out_spec],
        out_shape=[out_shape, out_shape],
        scratch_shapes=[pltpu.VMEM((N_STEPS, N_SLOTS, HEAD_DIM), F32)],
        compiler_params=_params("parallel"),
        name=f"attn_sample_d{dil}",
    )(q5, k5, v5, buf, ones)
    lse = jnp.pad(lse[..., 0], ((0, 0), (0, 0), (0, LANE - N_SLOTS)))
    return o.reshape(n_new * nseq, GROUP_COLS), lse.reshape(n_new * nseq, LANE)


def _cache_shift_kernel(main_ref, next_ref, new_ref, out_ref):
    rows = main_ref.shape[0]
    t = next_ref.shape[0]
    j = pl.program_id(2)
    out_ref[0:rows - t] = main_ref[t:rows]

    @pl.when(j < pl.num_programs(2) - 1)
    def _():
        out_ref[rows - t:rows] = next_ref[...]

    @pl.when(j == pl.num_programs(2) - 1)
    def _():
        out_ref[rows - t:rows] = new_ref[...]


def _cache_shift(cache, new, *, rows=512):
    depth, nseq, lb = cache.shape[:3]
    t = new.shape[2]
    rows = min(rows, lb)
    tile = cache.shape[3:]
    zeros = (0,) * len(tile)
    per_block = rows // t
    last_next = lb // t - 1
    return pl.pallas_call(
        _cache_shift_kernel,
        grid=(depth, nseq, lb // rows),
        in_specs=[pl.BlockSpec((None, None, rows) + tile, lambda l, b, j: (l, b, j) + zeros),
                  pl.BlockSpec((None, None, t) + tile,
                               lambda l, b, j: (l, b, jnp.minimum((j + 1) * per_block, last_next)) + zeros),
                  pl.BlockSpec((None, None, t) + tile, lambda l, b, j: (l, b, 0) + zeros)],
        out_specs=pl.BlockSpec((None, None, rows) + tile, lambda l, b, j: (l, b, j) + zeros),
        out_shape=jax.ShapeDtypeStruct(cache.shape, cache.dtype),
        compiler_params=_params("parallel", "parallel", "parallel"),
        name="cache_shift",
    )(cache, cache, new)


def _dwconv_kernel(x_ref, hist_ref, w_ref, b_ref, o_ref, ext_ref, *, width, shift, bps, chunk):
    tm = x_ref.shape[0]
    hist = hist_ref.shape[0]
    base = hist - (width - 1) * shift

    @pl.when(pl.program_id(1) % bps == 0)
    def _():
        ext_ref[0:hist, :] = hist_ref[...]

    ext_ref[hist:hist + tm, :] = x_ref[...]
    bias = b_ref[...]
    for r in range(0, tm, chunk):
        acc = jnp.broadcast_to(bias, (chunk, bias.shape[1]))
        if shift == 1:
            for res in range(SUBLANE):
                taps = list(range(res, width, SUBLANE))
                lo = base + r + res
                win = ext_ref[lo:lo + chunk + SUBLANE * (len(taps) - 1), :]
                for a, j in enumerate(taps):
                    acc = acc + w_ref[j:j + 1, :] * win[SUBLANE * a:SUBLANE * a + chunk]
        else:
            for j in range(width):
                lo = base + r + j * shift
                acc = acc + w_ref[j:j + 1, :] * ext_ref[lo:lo + chunk, :]
        o_ref[r:r + chunk, :] = acc
    if bps > 1:
        ext_ref[0:hist, :] = ext_ref[tm:tm + hist, :]


def _dwconv(x, hist, w, b, *, shift, tm=512, tc=256, chunk=32):
    m, c = x.shape
    nseq, hrows, _ = hist.shape
    width = w.shape[0]
    bps = m // nseq // tm
    return pl.pallas_call(
        functools.partial(_dwconv_kernel, width=width, shift=shift, bps=bps, chunk=chunk),
        grid=(c // tc, m // tm),
        in_specs=[pl.BlockSpec((tm, tc), lambda ci, i: (i, ci)),
                  pl.BlockSpec((None, hrows, tc), lambda ci, i: (i // bps, 0, ci)),
                  pl.BlockSpec((width, tc), lambda ci, i: (0, ci)),
                  pl.BlockSpec((1, tc), lambda ci, i: (0, ci))],
        out_specs=pl.BlockSpec((tm, tc), lambda ci, i: (i, ci)),
        out_shape=jax.ShapeDtypeStruct((m, c), F32),
        scratch_shapes=[pltpu.VMEM((hrows + tm, tc), F32)],
        compiler_params=_params("parallel", "arbitrary"),
        name="dwconv",
    )(x, hist, w, b)


def _merge_kernel(o0_ref, o1_ref, o2_ref, l0_ref, l1_ref, l2_ref, c_ref, ga_ref, gb_ref, x_ref,
                  wa_ref, wc_ref, wo_ref, lng_ref, lnb_ref, npost_ref, nffn_ref,
                  xo_ref, h2_ref, attn_ref):
    o_refs = (o0_ref, o1_ref, o2_ref)
    lses = [r[...] for r in (l0_ref, l1_ref, l2_ref)]
    mx = jnp.maximum(jnp.maximum(lses[0], lses[1]), lses[2])
    es = [jnp.exp(l - mx) for l in lses]
    inv = 1.0 / (es[0] + es[1] + es[2])
    wts = [e * inv for e in es]
    for h in range(N_SLOTS):
        sl = slice(h * HEAD_DIM, (h + 1) * HEAD_DIM)
        a = wts[0][:, h:h + 1] * o_refs[0][:, sl]
        a = a + wts[1][:, h:h + 1] * o_refs[1][:, sl]
        a = a + wts[2][:, h:h + 1] * o_refs[2][:, sl]
        attn_ref[:, sl] = a.astype(BF16)
    branch_a = _dot(attn_ref[...], wa_ref[...])

    c = c_ref[...]
    mu = jnp.mean(c, axis=-1, keepdims=True)
    cc = c - mu
    var = jnp.mean(cc * cc, axis=-1, keepdims=True)
    y = cc * lax.rsqrt(var + EPS) * lng_ref[...] + lnb_ref[...]
    branch_b = _dot((y * _sigmoid(y)).astype(BF16), wc_ref[...])

    merged = ga_ref[...].astype(F32) * branch_a + gb_ref[...].astype(F32) * branch_b
    mix = _dot(merged.astype(BF16), wo_ref[...])
    x_new = x_ref[...] + _rms(mix, npost_ref[...])
    xo_ref[...] = x_new
    h2_ref[...] = _rms(x_new, nffn_ref[...]).astype(BF16)


def _merge(os_, lses, c, gates, x, w_attn_out, w_conv_out, w_o, ln_g, ln_b, n_post, n_ffn, *, tm=256):
    m, dm = x.shape
    dc = c.shape[1]

    def row(width):
        return pl.BlockSpec((tm, width), lambda i: (i, 0))

    def const(shape):
        return pl.BlockSpec(shape, lambda i: (0,) * len(shape), pipeline_mode=pl.Buffered(1))

    in_specs = ([row(GROUP_COLS)] * 3 + [row(LANE)] * 3
                + [row(dc), row(dm), pl.BlockSpec((tm, dm), lambda i: (i, 1)),
                   row(dm), const(w_attn_out.shape), const(w_conv_out.shape), const(w_o.shape),
                   const((1, dc)), const((1, dc)), const((1, dm)), const((1, dm))])
    return pl.pallas_call(
        _merge_kernel,
        name="merge",
        grid=(m // tm,),
        in_specs=in_specs,
        out_specs=[row(dm), row(dm)],
        out_shape=[jax.ShapeDtypeStruct((m, dm), F32), jax.ShapeDtypeStruct((m, dm), BF16)],
        scratch_shapes=[pltpu.VMEM((tm, GROUP_COLS), BF16)],
        compiler_params=_params("parallel"),
    )(*os_, *lses, c, gates, gates, x, w_attn_out, w_conv_out, w_o, ln_g, ln_b, n_post, n_ffn)


def _gelu_tanh(x):
    return 0.5 * x * (1.0 + jnp.tanh(0.7978845608028654 * (x + 0.044715 * (x * x * x))))


def _ffn_kernel(*refs, shift, bps):
    (h_ref, x_ref, hist_ref, wa_ref, wv_ref, dw_ref, db_ref, wd_ref, g_ref, xo_ref, tail_ref, acc_ref, ext_ref) = refs[:13]
    carry_ref = refs[13] if bps > 1 else None
    i, j = pl.program_id(0), pl.program_id(1)
    tm = h_ref.shape[0]
    hist = hist_ref.shape[0]
    h = h_ref[...]
    a = _dot(h, wa_ref[...])
    val = _dot(h, wv_ref[...])
    if bps > 1:
        first = i % bps == 0

        @pl.when(first)
        def _():
            ext_ref[0:hist, :] = hist_ref[...]

        @pl.when(jnp.logical_not(first))
        def _():
            ext_ref[0:hist, :] = carry_ref[j]
    else:
        ext_ref[0:hist, :] = hist_ref[...]
    ext_ref[hist:hist + tm, :] = a
    ac = db_ref[...] + dw_ref[2:3, :] * a
    for tap in range(FFN_CONV_WIDTH - 1):
        lo = hist - (FFN_CONV_WIDTH - 1 - tap) * shift
        ac = ac + dw_ref[tap:tap + 1, :] * ext_ref[lo:lo + tm, :]
    tail = ext_ref[tm:tm + hist, :]
    tail_ref[...] = tail
    if bps > 1:
        carry_ref[j] = tail
    contrib = _dot((_gelu_tanh(ac) * val).astype(BF16), wd_ref[...])

    @pl.when(j == 0)
    def _():
        acc_ref[...] = contrib

    @pl.when(j > 0)
    def _():
        acc_ref[...] += contrib

    @pl.when(j == pl.num_programs(1) - 1)
    def _():
        xo_ref[...] = x_ref[...] + _rms(acc_ref[...], g_ref[...])


def _ffn(h2, x, hist, w_up, dw_w, dw_b, w_down, n_post, layer, *, shift, tm=512, tf=512):
    m, dm = x.shape
    nseq, hrows, dff = hist.shape
    bps = m // nseq // tm
    nf = dff // tf
    scratch = [pltpu.VMEM((tm, dm), F32), pltpu.VMEM((hrows + tm, tf), F32)]
    if bps > 1:
        scratch.append(pltpu.VMEM((nf, hrows, tf), F32))
    return pl.pallas_call(
        functools.partial(_ffn_kernel, shift=shift, bps=bps),
        grid=(m // tm, nf),
        in_specs=[pl.BlockSpec((tm, dm), lambda i, j: (i, 0)),
                  pl.BlockSpec((tm, dm), lambda i, j: (i, 0)),
                  pl.BlockSpec((None, hrows, tf), lambda i, j: (i // bps, 0, j)),
                  pl.BlockSpec((None, dm, tf), lambda i, j: (layer, 0, j)),
                  pl.BlockSpec((None, dm, tf), lambda i, j: (layer, 0, nf + j)),
                  pl.BlockSpec((FFN_CONV_WIDTH, tf), lambda i, j: (0, j)),
                  pl.BlockSpec((1, tf), lambda i, j: (0, j)),
                  pl.BlockSpec((None, tf, dm), lambda i, j: (layer, j, 0)),
                  pl.BlockSpec((1, dm), lambda i, j: (0, 0))],
        out_specs=[pl.BlockSpec((tm, dm), lambda i, j: (i, 0)),
                   pl.BlockSpec((None, hrows, tf), lambda i, j: (i, 0, j))],
        out_shape=[jax.ShapeDtypeStruct((m, dm), F32), jax.ShapeDtypeStruct((m // tm, hrows, dff), F32)],
        scratch_shapes=scratch,
        name="ffn",
        compiler_params=_params("arbitrary", "arbitrary"),
    )(h2, x, hist, w_up, w_up, dw_w, dw_b, w_down, n_post)


def _ple_kernel(x_ref, p_ref, wg_ref, wp_ref, o_ref):
    x = x_ref[...]
    gate = _sigmoid(_dot(x.astype(BF16), wg_ref[...]))
    o_ref[...] = x + gate * _dot(p_ref[...].astype(BF16), wp_ref[...])


def _ple(x, p, w_gate, w_proj, *, tm=512):
    m, dm = x.shape
    dp = p.shape[1]
    return pl.pallas_call(
        _ple_kernel,
        grid=(m // tm,),
        in_specs=[pl.BlockSpec((tm, dm), lambda i: (i, 0)), pl.BlockSpec((tm, dp), lambda i: (i, 0)),
                  pl.BlockSpec((dm, dm), lambda i: (0, 0), pipeline_mode=pl.Buffered(1)),
                  pl.BlockSpec((dp, dm), lambda i: (0, 0), pipeline_mode=pl.Buffered(1))],
        out_specs=pl.BlockSpec((tm, dm), lambda i: (i, 0)),
        out_shape=jax.ShapeDtypeStruct((m, dm), F32),
        compiler_params=_params("parallel"),
        name="ple",
    )(x, p, w_gate, w_proj)


def _rope_tables(pos):
    half = HEAD_DIM // 2
    inv = ROPE_THETA ** (-jnp.arange(half, dtype=F32) / half)
    ang = pos.astype(F32)[:, None] * inv[None, :]
    cos, sin = jnp.cos(ang), jnp.sin(ang)
    return jnp.concatenate([cos, cos], axis=-1), jnp.concatenate([-sin, sin], axis=-1)


def _layer(x, ple, rope, conv_hist, ffn_hist, wts, layer, *, shift, attend):
    qkv = N_GROUPS * GROUP_COLS
    dconv = wts["conv_dw_w"].shape[-1]
    dm = x.shape[1]
    gain = wts["norm_mix_pre"][layer][None]
    proj = functools.partial(_in_proj, x, gain, wts["w_in"], layer)
    q = proj(0, qkv, epilogue="rope", out_dtype=F32, rope=rope, scale=HEAD_DIM ** -0.5)
    k = proj(qkv, qkv, epilogue="rope", out_dtype=F32, rope=rope)
    v = proj(2 * qkv, qkv, epilogue="plain", out_dtype=F32)
    glu = proj(3 * qkv, dconv, epilogue="glu", out_dtype=F32)
    gates = proj(3 * qkv + 2 * dconv, 2 * dm, epilogue="sigmoid", out_dtype=BF16)

    os_, lses = zip(*(attend(q, k, v, gi) for gi in range(N_GROUPS)))
    c = _dwconv(glu, conv_hist, wts["conv_dw_w"][layer], wts["conv_dw_b"][layer][None], shift=shift)
    x, h2 = _merge(os_, lses, c, gates, x, wts["w_attn_out"][layer], wts["w_conv_out"][layer], wts["w_o"][layer],
                   wts["conv_ln_g"][layer][None], wts["conv_ln_b"][layer][None],
                   wts["norm_mix_post"][layer][None], wts["norm_ffn_pre"][layer][None])
    x, ffn_tail = _ffn(h2, x, ffn_hist, wts["w_ffn_up"], wts["ffn_dw_w"][layer], wts["ffn_dw_b"][layer][None],
                       wts["w_ffn_down"], wts["norm_ffn_post"][layer][None], layer, shift=shift)
    x = _ple(x, ple, wts["w_ple_gate"][layer], wts["w_ple_proj"][layer])
    return x, k, v, glu, ffn_tail


def kernel(x_prompt, x_sample, p_prompt, p_sample, cache_kv_w128, cache_kv_w512, cache_kv_w2048, state_conv, state_ffn_conv, w_in, w_attn_out, conv_dw_w, conv_dw_b, conv_ln_g, conv_ln_b, w_conv_out, w_o, norm_mix_pre, norm_mix_post, norm_ffn_pre, norm_ffn_post, w_ffn_up, ffn_dw_w, ffn_dw_b, w_ffn_down, w_ple_gate, w_ple_proj):
    depth = w_in.shape[0]
    nb_p, seq, dm = x_prompt.shape
    nb_s, t_new, _ = x_sample.shape
    caches = (cache_kv_w128, cache_kv_w512, cache_kv_w2048)
    past = cache_kv_w2048.shape[2]
    dconv = conv_dw_w.shape[-1]
    dff = ffn_dw_w.shape[-1]
    conv_hist_rows = -(-(CONV_WIDTH - 1) // SUBLANE) * SUBLANE
    wts = dict(
        w_in=w_in.astype(BF16), w_attn_out=w_attn_out.astype(BF16), w_conv_out=w_conv_out.astype(BF16),
        w_o=w_o.astype(BF16), w_ffn_up=w_ffn_up.astype(BF16), w_ffn_down=w_ffn_down.astype(BF16),
        w_ple_gate=w_ple_gate.astype(BF16), w_ple_proj=w_ple_proj.astype(BF16),
        conv_dw_w=conv_dw_w, conv_dw_b=conv_dw_b, conv_ln_g=conv_ln_g, conv_ln_b=conv_ln_b,
        norm_mix_pre=norm_mix_pre, norm_mix_post=norm_mix_post, norm_ffn_pre=norm_ffn_pre,
        norm_ffn_post=norm_ffn_post, ffn_dw_w=ffn_dw_w, ffn_dw_b=ffn_dw_b)

    rope_p = _rope_tables(jnp.tile(jnp.arange(seq, dtype=jnp.int32), nb_p))
    rope_s = _rope_tables(jnp.repeat(past + jnp.arange(t_new, dtype=jnp.int32), nb_s))
    ones = jnp.ones((LANE, LANE), BF16)

    xp = x_prompt.reshape(nb_p * seq, dm)
    xs = x_sample.transpose(1, 0, 2).reshape(t_new * nb_s, dm)
    zero_conv = jnp.zeros((nb_p, conv_hist_rows, dconv), F32)
    zero_ffn = jnp.zeros((nb_p, SUBLANE, dff), F32)

    kv_p = [[] for _ in DIL_GROUPS]
    kv_s = [[] for _ in DIL_GROUPS]
    conv_p, conv_s, ffn_p, ffn_s = [], [], [], []
    for i in range(depth):
        attend_p = lambda q, k, v, gi: _attn_prompt(q, k, v, gi, nb_p)
        xp, k, v, glu, tail = _layer(xp, p_prompt[i].reshape(nb_p * seq, -1), rope_p, zero_conv, zero_ffn, wts, i,
                                     shift=1, attend=attend_p)
        k5 = k.reshape(nb_p, seq, N_GROUPS, N_SLOTS, HEAD_DIM)
        v5 = v.reshape(nb_p, seq, N_GROUPS, N_SLOTS, HEAD_DIM)
        for gi, (win, _) in enumerate(DIL_GROUPS):
            keep = min(win, seq)
            kv_p[gi].append(jnp.stack([k5[:, seq - keep:, gi], v5[:, seq - keep:, gi]], axis=2))
        conv_p.append(glu.reshape(nb_p, seq, dconv)[:, seq - (CONV_WIDTH - 1):])
        ffn_p.append(tail.reshape(nb_p, -1, SUBLANE, dff)[:, -1, SUBLANE - (FFN_CONV_WIDTH - 1):])

        attend_s = lambda q, k, v, gi, i=i: _attn_sample(q, k, v, caches[gi], i, gi, ones)
        conv_hist = state_conv[i].transpose(1, 0, 2).reshape(1, (CONV_WIDTH - 1) * nb_s, dconv)
        ffn_hist = state_ffn_conv[i].transpose(1, 0, 2).reshape(1, (FFN_CONV_WIDTH - 1) * nb_s, dff)
        xs, k, v, glu, tail = _layer(xs, p_sample[i].transpose(1, 0, 2).reshape(t_new * nb_s, -1), rope_s,
                                     conv_hist, ffn_hist, wts, i, shift=nb_s, attend=attend_s)
        k5 = k.reshape(t_new, nb_s, N_GROUPS, N_SLOTS, HEAD_DIM)
        v5 = v.reshape(t_new, nb_s, N_GROUPS, N_SLOTS, HEAD_DIM)
        for gi in range(N_GROUPS):
            kv_s[gi].append(jnp.stack([k5[:, :, gi], v5[:, :, gi]], axis=2).transpose(1, 0, 2, 3, 4))
        glu_bt = glu.reshape(t_new, nb_s, dconv).transpose(1, 0, 2)
        conv_s.append(jnp.concatenate([state_conv[i][:, t_new:], glu_bt], axis=1))
        ffn_s.append(tail.reshape(FFN_CONV_WIDTH - 1, nb_s, dff).transpose(1, 0, 2))

    kv_s_out = [_cache_shift(cache, jnp.stack(rows)) for cache, rows in zip(caches, kv_s)]
    return (xp.reshape(nb_p, seq, dm), xs.reshape(t_new, nb_s, dm).transpose(1, 0, 2),
            jnp.stack(kv_p[0]), jnp.stack(kv_p[1]), jnp.stack(kv_p[2]), jnp.stack(conv_p), jnp.stack(ffn_p),
            kv_s_out[0], kv_s_out[1], kv_s_out[2], jnp.stack(conv_s), jnp.stack(ffn_s))
```
